```python
import math
import jax, jax.numpy as jnp
from jax import lax
import numpy as np

D_MODEL = 1024
BATCH = 32
SEQ = 2048
DEPTH = 4
DEC_BATCH = 8
DEC_SEQ = 2048
PAST_LEN = 128

HEAD_DIM = 64
N_Q_HEADS = 8
N_KV_HEADS = 2
Q_PER_KV = N_Q_HEADS // N_KV_HEADS
ATTN_WIDTH = N_Q_HEADS * HEAD_DIM
KV_WIDTH = N_KV_HEADS * HEAD_DIM
WINDOW = 128
ATTN_BLOCK = 128
KEY_SPAN = ATTN_BLOCK + 2 * WINDOW
ROPE_THETA = 10000.0
SSD_HEADS = 16
SSD_HEAD_DIM = 64
SSD_WIDTH = SSD_HEADS * SSD_HEAD_DIM
SSD_GROUPS = 2
SSD_HEADS_PER_GROUP = SSD_HEADS // SSD_GROUPS
SSD_STATE = 128
SSD_CONV = 5
SSD_CHUNK = 128
XBC_WIDTH = SSD_WIDTH + 2 * SSD_GROUPS * SSD_STATE
MIX_WIDTH = ATTN_WIDTH + SSD_WIDTH
IN_PROJ_WIDTH = ATTN_WIDTH + 2 * KV_WIDTH + SSD_WIDTH + XBC_WIDTH + 2 * SSD_HEADS
D_FF = 2816
FFN_CONV = 3
DEEPNORM_ALPHA = (2 * DEPTH) ** 0.25
DEEPNORM_BETA = (8 * DEPTH) ** -0.25
EPS = 1e-5
MASK_VALUE = -1e30

kernel_name = "hymba_style_bidir_ssd_swa_encoder"


def layer_norm(x, g, b):
    xf = x.astype(jnp.float32)
    mu = jnp.mean(xf, axis=-1, keepdims=True)
    var = jnp.mean(jnp.square(xf - mu), axis=-1, keepdims=True)
    return ((xf - mu) * lax.rsqrt(var + EPS) * g.astype(jnp.float32) + b.astype(jnp.float32)).astype(x.dtype)


def rms_norm(x, g):
    xf = x.astype(jnp.float32)
    ms = jnp.mean(jnp.square(xf), axis=-1, keepdims=True)
    return (xf * lax.rsqrt(ms + EPS) * g.astype(jnp.float32)).astype(x.dtype)


def depthwise_conv(x, w, b):
    pad = (w.shape[0] - 1) // 2
    y = lax.conv_general_dilated(
        x, w[:, None, :].astype(x.dtype), window_strides=(1,), padding=[(pad, pad)],
        dimension_numbers=("NWC", "WIO", "NWC"), feature_group_count=x.shape[-1])
    return y + b.astype(x.dtype)


def rope(x):
    s, d = x.shape[1], x.shape[-1]
    inv_freq = ROPE_THETA ** (-jnp.arange(0, d, 2, dtype=jnp.float32) / d)
    ang = jnp.arange(s, dtype=jnp.float32)[:, None] * inv_freq[None, :]
    cos = jnp.cos(ang)[None, :, None, :]
    sin = jnp.sin(ang)[None, :, None, :]
    xf = x.astype(jnp.float32)
    x1, x2 = xf[..., : d // 2], xf[..., d // 2:]
    return jnp.concatenate([x1 * cos - x2 * sin, x2 * cos + x1 * sin], axis=-1).astype(x.dtype)


def window_attention(q, k, v, sink):
    bsz, s = q.shape[0], q.shape[1]
    nb = s // ATTN_BLOCK
    scale = HEAD_DIM ** -0.5
    kp = jnp.pad(k, ((0, 0), (WINDOW, WINDOW), (0, 0), (0, 0)))
    vp = jnp.pad(v, ((0, 0), (WINDOW, WINDOW), (0, 0), (0, 0)))
    qb = q.reshape(bsz, nb, ATTN_BLOCK, N_KV_HEADS, Q_PER_KV, HEAD_DIM).transpose(1, 0, 2, 3, 4, 5)
    sink_b = sink.astype(jnp.float32).reshape(N_KV_HEADS, Q_PER_KV)[None, :, :, None, None]

    def one_block(args):
        i, q_blk = args
        start = i * ATTN_BLOCK
        k_blk = lax.dynamic_slice_in_dim(kp, start, KEY_SPAN, axis=1)
        v_blk = lax.dynamic_slice_in_dim(vp, start, KEY_SPAN, axis=1)
        sc = jnp.einsum("bqhgd,bkhd->bhgqk", q_blk, k_blk).astype(jnp.float32) * scale
        q_pos = start + jnp.arange(ATTN_BLOCK)
        k_pos = start - WINDOW + jnp.arange(KEY_SPAN)
        mask = (jnp.abs(q_pos[:, None] - k_pos[None, :]) <= WINDOW) & (k_pos[None, :] >= 0) & (k_pos[None, :] < s)
        sc = jnp.where(mask, sc, MASK_VALUE)
        m = jnp.maximum(jnp.max(sc, axis=-1, keepdims=True), sink_b)
        p = jnp.exp(sc - m)
        denom = jnp.sum(p, axis=-1, keepdims=True) + jnp.exp(sink_b - m)
        probs = (p / denom).astype(v_blk.dtype)
        return jnp.einsum("bhgqk,bkhd->bqhgd", probs, v_blk)

    out = lax.map(one_block, (jnp.arange(nb), qb))
    return out.transpose(1, 0, 2, 3, 4, 5).reshape(bsz, s, ATTN_WIDTH)


def ssd_chunked(x, dt, a, bm, cm):
    bsz, s = x.shape[0], x.shape[1]
    nc = s // SSD_CHUNK
    G, E, P, N, Q = SSD_GROUPS, SSD_HEADS_PER_GROUP, SSD_HEAD_DIM, SSD_STATE, SSD_CHUNK
    xd = (x * dt[..., None]).reshape(bsz, nc, Q, G, E, P)
    la = (dt * a).reshape(bsz, nc, Q, G, E).transpose(0, 1, 3, 4, 2)
    a_cs = jnp.cumsum(la, axis=-1)
    bc = bm.reshape(bsz, nc, Q, G, N)
    cc = cm.reshape(bsz, nc, Q, G, N)
    idx = jnp.arange(Q)
    lower = idx[:, None] >= idx[None, :]
    seg = a_cs[..., :, None] - a_cs[..., None, :]
    decay = jnp.where(lower, jnp.exp(jnp.where(lower, seg, 0.0)), 0.0)
    cb = jnp.einsum("bcqgn,bcsgn->bcgqs", cc, bc)
    y_diag = jnp.einsum("bcgeqs,bcsgep->bcqgep", cb[:, :, :, None] * decay, xd)
    decay_to_end = jnp.exp(a_cs[..., -1:] - a_cs)
    states = jnp.einsum("bcsgn,bcges,bcsgep->bcgepn", bc, decay_to_end, xd)
    chunk_decay = jnp.exp(a_cs[..., -1])

    def step(h, inp):
        st, dec = inp
        return dec[..., None, None] * h + st, h

    _, prev = lax.scan(step, jnp.zeros_like(states[:, 0]),
                       (jnp.moveaxis(states, 1, 0), jnp.moveaxis(chunk_decay, 1, 0)))
    y_off = jnp.einsum("bcqgn,cbgepn,bcgeq->bcqgep", cc, prev, jnp.exp(a_cs))
    return (y_diag + y_off).reshape(bsz, s, SSD_HEADS, P)


def bidir_ssd(xs, dt_raw, dt_bias, a_log, bm, cm):
    dt = jax.nn.softplus(dt_raw.astype(jnp.float32) + dt_bias.astype(jnp.float32))
    a = -jnp.exp(a_log.astype(jnp.float32))
    flip = lambda t: jnp.flip(t, axis=1)
    y_f = ssd_chunked(xs, dt[:, :, 0], a[0], bm, cm)
    y_b = flip(ssd_chunked(flip(xs), flip(dt[:, :, 1]), a[1], flip(bm), flip(cm)))
    return y_f + y_b


def hybrid_layer(h, w_in, attn_sink, ssd_conv_w, ssd_conv_b, ssd_a_log, ssd_dt_bias, ssd_d,
                 ssd_norm_g, w_out, ln1_g, ln1_b, w_up, ffn_conv_w, ffn_conv_b, w_down, ln2_g, ln2_b):
    bsz, s = h.shape[0], h.shape[1]
    proj = h @ w_in
    splits = np.cumsum([ATTN_WIDTH, KV_WIDTH, KV_WIDTH, SSD_WIDTH, XBC_WIDTH])
    q, k, v, z, xbc, dt_raw = jnp.split(proj, splits, axis=-1)
    q = rope(q.reshape(bsz, s, N_Q_HEADS, HEAD_DIM))
    k = rope(k.reshape(bsz, s, N_KV_HEADS, HEAD_DIM))
    v = v.reshape(bsz, s, N_KV_HEADS, HEAD_DIM)
    attn_out = window_attention(q, k, v, attn_sink)
    xbc = jax.nn.silu(depthwise_conv(xbc, ssd_conv_w, ssd_conv_b))
    xs, bm, cm = jnp.split(xbc, [SSD_WIDTH, SSD_WIDTH + SSD_GROUPS * SSD_STATE], axis=-1)
    xs = xs.reshape(bsz, s, SSD_HEADS, SSD_HEAD_DIM)
    bm = bm.reshape(bsz, s, SSD_GROUPS, SSD_STATE)
    cm = cm.reshape(bsz, s, SSD_GROUPS, SSD_STATE)
    y = bidir_ssd(xs, dt_raw.reshape(bsz, s, 2, SSD_HEADS), ssd_dt_bias, ssd_a_log, bm, cm)
    y = (y + ssd_d[:, None] * xs).reshape(bsz, s, SSD_WIDTH).astype(h.dtype)
    ssd_out = rms_norm(y * jax.nn.silu(z), ssd_norm_g)
    mix = jnp.concatenate([attn_out, ssd_out], axis=-1) @ w_out
    h = layer_norm(DEEPNORM_ALPHA * h + mix, ln1_g, ln1_b)
    gate, val = jnp.split(h @ w_up, 2, axis=-1)
    gate = depthwise_conv(gate, ffn_conv_w, ffn_conv_b)
    ffn = (jax.nn.silu(gate) * val) @ w_down
    return layer_norm(DEEPNORM_ALPHA * h + ffn, ln2_g, ln2_b)


def setup_inputs(seed: int = 0) -> dict:
    key = jax.random.key(seed)
    ks = jax.random.split(key, 24)
    f32 = jnp.float32
    nrm = lambda k, shape: jax.random.normal(k, shape, dtype=f32)
    dt0 = jnp.exp(jax.random.uniform(ks[9], (DEPTH, 2, SSD_HEADS), dtype=f32,
                                     minval=math.log(1e-3), maxval=math.log(1e-1)))
    return {
        "x_prompt": nrm(ks[0], (BATCH, SEQ, D_MODEL)),
        "x_sample": nrm(ks[1], (DEC_BATCH, DEC_SEQ, D_MODEL)),
        "ln_in_g": 1.0 + 0.02 * nrm(ks[2], (D_MODEL,)),
        "ln_in_b": 0.02 * nrm(ks[3], (D_MODEL,)),
        "w_in": nrm(ks[4], (DEPTH, D_MODEL, IN_PROJ_WIDTH)) * D_MODEL ** -0.5,
        "attn_sink": 0.5 * nrm(ks[5], (DEPTH, N_Q_HEADS)),
        "ssd_conv_w": nrm(ks[6], (DEPTH, SSD_CONV, XBC_WIDTH)) * SSD_CONV ** -0.5,
        "ssd_conv_b": 0.02 * nrm(ks[7], (DEPTH, XBC_WIDTH)),
        "ssd_a_log": jnp.log(jax.random.uniform(ks[8], (DEPTH, 2, SSD_HEADS), dtype=f32, minval=1.0, maxval=16.0)),
        "ssd_dt_bias": dt0 + jnp.log(-jnp.expm1(-dt0)),
        "ssd_d": 1.0 + 0.02 * nrm(ks[10], (DEPTH, SSD_HEADS)),
        "ssd_norm_g": 1.0 + 0.02 * nrm(ks[11], (DEPTH, SSD_WIDTH)),
        "w_out": nrm(ks[12], (DEPTH, MIX_WIDTH, D_MODEL)) * (MIX_WIDTH ** -0.5 * DEEPNORM_BETA),
        "ln1_g": 1.0 + 0.02 * nrm(ks[13], (DEPTH, D_MODEL)),
        "ln1_b": 0.02 * nrm(ks[14], (DEPTH, D_MODEL)),
        "w_up": nrm(ks[15], (DEPTH, D_MODEL, 2 * D_FF)) * D_MODEL ** -0.5,
        "ffn_conv_w": nrm(ks[16], (DEPTH, FFN_CONV, D_FF)) * FFN_CONV ** -0.5,
        "ffn_conv_b": 0.02 * nrm(ks[17], (DEPTH, D_FF)),
        "w_down": nrm(ks[18], (DEPTH, D_FF, D_MODEL)) * (D_FF ** -0.5 * DEEPNORM_BETA),
        "ln2_g": 1.0 + 0.02 * nrm(ks[19], (DEPTH, D_MODEL)),
        "ln2_b": 0.02 * nrm(ks[20], (DEPTH, D_MODEL)),
    }


def reference(x_prompt, x_sample, ln_in_g, ln_in_b, w_in, attn_sink, ssd_conv_w, ssd_conv_b,
              ssd_a_log, ssd_dt_bias, ssd_d, ssd_norm_g, w_out, ln1_g, ln1_b, w_up, ffn_conv_w,
              ffn_conv_b, w_down, ln2_g, ln2_b):
    def trunk(x):
        h = layer_norm(x, ln_in_g, ln_in_b)
        for i in range(DEPTH):
            h = hybrid_layer(h, w_in[i], attn_sink[i], ssd_conv_w[i], ssd_conv_b[i], ssd_a_log[i],
                             ssd_dt_bias[i], ssd_d[i], ssd_norm_g[i], w_out[i], ln1_g[i], ln1_b[i],
                             w_up[i], ffn_conv_w[i], ffn_conv_b[i], w_down[i], ln2_g[i], ln2_b[i])
        return h

    y_prompt = trunk(x_prompt)
    y_sample = trunk(x_sample)
    return (y_prompt, y_sample)
```

```python
import functools
import math

import jax
import jax.numpy as jnp
import numpy as np
from jax import lax
from jax.experimental import pallas as pl
from jax.experimental.pallas import tpu as pltpu

F32 = jnp.float32
BF16 = jnp.bfloat16

D_MODEL = 1024
DEPTH = 4
HEAD_DIM = 64
N_Q_HEADS = 8
N_KV_HEADS = 2
ATTN_WIDTH = N_Q_HEADS * HEAD_DIM
KV_WIDTH = N_KV_HEADS * HEAD_DIM
WINDOW = 128
ROPE_THETA = 10000.0
SSD_HEADS = 16
SSD_HEAD_DIM = 64
SSD_WIDTH = SSD_HEADS * SSD_HEAD_DIM
SSD_GROUPS = 2
HEADS_PER_GROUP = SSD_HEADS // SSD_GROUPS
GROUP_WIDTH = HEADS_PER_GROUP * SSD_HEAD_DIM
SSD_STATE = 128
SSD_CONV = 5
CHUNK = 128
XBC_WIDTH = SSD_WIDTH + 2 * SSD_GROUPS * SSD_STATE
D_FF = 2816
FFN_CONV = 3
DEEPNORM_ALPHA = (2 * DEPTH) ** 0.25
EPS = 1e-5
MASK_VALUE = -1e30

LANES = 128
VMEM_LIMIT = 56 * 1024 * 1024
TOKEN_TILE = 512
ATTN_BQ = 256
FF_TILE = 256
FF_HALO = 16


def _dot(a, b):
    return jnp.dot(a, b, preferred_element_type=F32)


def _dot_nt(a, b):
    return lax.dot_general(a, b, (((1,), (1,)), ((), ())), preferred_element_type=F32)


def _dot_tn(a, b):
    return lax.dot_general(a, b, (((0,), (0,)), ((), ())), preferred_element_type=F32)


def _layer_norm(x, g, b):
    mu = jnp.mean(x, axis=-1, keepdims=True)
    xc = x - mu
    var = jnp.mean(xc * xc, axis=-1, keepdims=True)
    return xc * lax.rsqrt(var + EPS) * g + b


def _silu(x):
    return x * (1.0 / (1.0 + jnp.exp(-x)))


def _split3(x):
    hi = x.astype(BF16)
    r1 = x - hi.astype(F32)
    mid = r1.astype(BF16)
    lo = (r1 - mid.astype(F32)).astype(BF16)
    return jnp.concatenate([hi, mid, lo], axis=1)


def _inproj_kernel(apply_ln, x_ref, lng_ref, lnb_ref, wqk_ref, wv_ref, wz_ref, wx_ref, wdt_ref,
                   cos_ref, sin_ref, *outs):
    if apply_ln:
        h_ref, q_ref, kv_ref, z_ref, xbc_ref, dt_ref = outs
    else:
        q_ref, kv_ref, z_ref, xbc_ref, dt_ref = outs
    x = x_ref[...]
    if apply_ln:
        x = _layer_norm(x, lng_ref[...], lnb_ref[...])
        h_ref[...] = x
    xb = x.astype(BF16)
    tm = x.shape[0]
    cos = cos_ref[...]
    sin = sin_ref[...]
    lane = lax.broadcasted_iota(jnp.int32, (tm, LANES), 1)
    first_half = (lane % HEAD_DIM) < (HEAD_DIM // 2)
    low_head = lane < HEAD_DIM

    def rope(t):
        swapped = jnp.where(first_half, pltpu.roll(t, LANES - HEAD_DIM // 2, 1), pltpu.roll(t, HEAD_DIM // 2, 1))
        return t * cos + swapped * sin

    def both_halves(t):
        sw = pltpu.roll(t, HEAD_DIM, 1)
        return jnp.where(low_head, t, sw), jnp.where(low_head, sw, t)

    qk = _dot(xb, wqk_ref[...])
    scale = HEAD_DIM ** -0.5
    for j in range(ATTN_WIDTH // LANES):
        q_ref[:, j * LANES:(j + 1) * LANES] = (rope(qk[:, j * LANES:(j + 1) * LANES]) * scale).astype(BF16)
    k0, k1 = both_halves(rope(qk[:, ATTN_WIDTH:ATTN_WIDTH + KV_WIDTH]))
    v0, v1 = both_halves(_dot(xb, wv_ref[...]))
    kv_ref[:, 0 * LANES:1 * LANES] = k0.astype(BF16)
    kv_ref[:, 1 * LANES:2 * LANES] = k1.astype(BF16)
    kv_ref[:, 2 * LANES:3 * LANES] = v0.astype(BF16)
    kv_ref[:, 3 * LANES:4 * LANES] = v1.astype(BF16)
    z_ref[...] = _dot(xb, wz_ref[...]).astype(BF16)
    xbc_ref[...] = _dot(xb, wx_ref[...]).astype(BF16)
    dt_ref[...] = _dot(xb, wdt_ref[...])


def _inproj(x2d, seq, apply_ln, lng, lnb, wqk, wv, wz, wx, wdt, cos_t, sin_t):
    t = x2d.shape[0]
    tm = TOKEN_TILE
    assert t % tm == 0 and seq % tm == 0
    tiles_per_seq = seq // tm
    row = lambda w: pl.BlockSpec((tm, w), lambda i: (i, 0))
    full = lambda a: pl.BlockSpec(a.shape, lambda i: (0,) * a.ndim)
    tab = pl.BlockSpec((tm, LANES), lambda i: (i % tiles_per_seq, 0))
    out_shape = [jax.ShapeDtypeStruct((t, ATTN_WIDTH), BF16), jax.ShapeDtypeStruct((t, 4 * LANES), BF16),
                 jax.ShapeDtypeStruct((t, SSD_WIDTH), BF16), jax.ShapeDtypeStruct((t, XBC_WIDTH), BF16),
                 jax.ShapeDtypeStruct((t, SSD_GROUPS * LANES), F32)]
    out_specs = [row(ATTN_WIDTH), row(4 * LANES), row(SSD_WIDTH), row(XBC_WIDTH), row(SSD_GROUPS * LANES)]
    if apply_ln:
        out_shape = [jax.ShapeDtypeStruct((t, D_MODEL), F32)] + out_shape
        out_specs = [row(D_MODEL)] + out_specs
    return pl.pallas_call(
        functools.partial(_inproj_kernel, apply_ln),
        grid=(t // tm,),
        in_specs=[row(D_MODEL), full(lng), full(lnb), full(wqk), full(wv), full(wz), full(wx), full(wdt), tab, tab],
        out_specs=out_specs,
        out_shape=out_shape,
        compiler_params=pltpu.CompilerParams(dimension_semantics=("arbitrary",), vmem_limit_bytes=VMEM_LIMIT),
        name="in_proj_ln" if apply_ln else "in_proj",
    )(x2d, lng, lnb, wqk, wv, wz, wx, wdt, cos_t, sin_t)


def _attn_kernel(seq, sink_ref, q_ref, kl_ref, km_ref, kr_ref, vl_ref, vm_ref, vr_ref, o_ref):
    bq = q_ref.shape[1]
    ks = bq + 2 * WINDOW
    start = pl.program_id(1) * bq
    qi = lax.broadcasted_iota(jnp.int32, (bq, ks), 0)
    kc = lax.broadcasted_iota(jnp.int32, (bq, ks), 1) - WINDOW
    d = kc - qi
    kpos = kc + start
    allow = (jnp.abs(d) <= WINDOW) & (kpos >= 0) & (kpos < seq)
    lane = lax.broadcasted_iota(jnp.int32, (bq, LANES), 1)
    low_head = lane < HEAD_DIM
    group = N_Q_HEADS // N_KV_HEADS
    for h in range(N_KV_HEADS):
        ksl = slice(h * LANES, (h + 1) * LANES)
        kk = jnp.concatenate([kl_ref[0, :, ksl], km_ref[0, :, ksl], kr_ref[0, :, ksl]], axis=0)
        vv = jnp.concatenate([vl_ref[0, :, ksl], vm_ref[0, :, ksl], vr_ref[0, :, ksl]], axis=0)
        qs = []
        for r in range(group):
            head = group * h + r
            qt = q_ref[0, :, (head // 2) * LANES:(head // 2 + 1) * LANES]
            keep = low_head if head % 2 == 0 else jnp.logical_not(low_head)
            qs.append(jnp.where(keep, qt, jnp.zeros_like(qt)))
        sc = _dot_nt(jnp.concatenate(qs, axis=0), kk)
        ps, inv = [], []
        for r in range(group):
            sink = sink_ref[group * h + r]
            s = jnp.where(allow, sc[r * bq:(r + 1) * bq], MASK_VALUE)
            m = jnp.maximum(jnp.max(s, axis=-1, keepdims=True), sink)
            p = jnp.exp(s - m)
            denom = jnp.sum(p, axis=-1, keepdims=True) + jnp.exp(sink - m)
            ps.append(p.astype(BF16))
            inv.append(1.0 / denom)
        pv = _dot(jnp.concatenate(ps, axis=0), vv)
        for pair in range(group // 2):
            even = pv[(2 * pair) * bq:(2 * pair + 1) * bq] * inv[2 * pair]
            odd = pv[(2 * pair + 1) * bq:(2 * pair + 2) * bq] * inv[2 * pair + 1]
            col = (group * h) // 2 + pair
            o_ref[0, :, col * LANES:(col + 1) * LANES] = jnp.where(low_head, even, odd).astype(BF16)


def _attention(q, kv, sink):
    b, seq, _ = q.shape
    bq = ATTN_BQ
    assert seq % bq == 0 and bq % WINDOW == 0
    r = bq // WINDOW
    nwb = seq // WINDOW
    side = lambda lane_blk, f: pl.BlockSpec((1, WINDOW, 2 * LANES), lambda bi, i: (bi, f(i), lane_blk))
    mid = lambda lane_blk: pl.BlockSpec((1, bq, 2 * LANES), lambda bi, i: (bi, i, lane_blk))
    left = lambda i: jnp.maximum(i * r - 1, 0)
    right = lambda i: jnp.minimum((i + 1) * r, nwb - 1)
    return pl.pallas_call(
        functools.partial(_attn_kernel, seq),
        grid=(b, seq // bq),
        in_specs=[pl.BlockSpec(memory_space=pltpu.SMEM),
                  pl.BlockSpec((1, bq, ATTN_WIDTH), lambda bi, i: (bi, i, 0)),
                  side(0, left), mid(0), side(0, right), side(1, left), mid(1), side(1, right)],
        out_specs=pl.BlockSpec((1, bq, ATTN_WIDTH), lambda bi, i: (bi, i, 0)),
        out_shape=jax.ShapeDtypeStruct((b, seq, ATTN_WIDTH), BF16),
        compiler_params=pltpu.CompilerParams(dimension_semantics=("arbitrary", "arbitrary"),
                                             vmem_limit_bytes=VMEM_LIMIT),
        name="window_attn",
    )(sink, q, kv, kv, kv, kv, kv, kv)


def _ssd_kernel(x_ref, bm_ref, cm_ref, z_ref, dt_ref, cwx_ref, cwb_ref, cwc_ref, cbx_ref, cbb_ref, cbc_ref,
                gp_ref, ng_ref, tril_ref, triu_ref, ef_ref, eb_ref, y_ref, ss_ref,
                pad_sc, xs_sc, bs_sc, cs_sc, yacc_sc, sf_sc, sb_sc):
    seq = x_ref.shape[1]
    nchunks = seq // CHUNK
    gw = GROUP_WIDTH
    hp = HEADS_PER_GROUP
    pad = 8

    zeros8 = jnp.zeros((pad, gw + 2 * LANES), F32)
    pad_sc[0:pad, :] = zeros8
    pad_sc[pad + seq:pad + seq + pad, :] = zeros8

    def fill(c, carry):
        r0 = pl.multiple_of(c * CHUNK, CHUNK)
        pad_sc[pl.ds(pad + r0, CHUNK), 0:gw] = x_ref[0, pl.ds(r0, CHUNK), :].astype(F32)
        pad_sc[pl.ds(pad + r0, CHUNK), gw:gw + LANES] = bm_ref[0, pl.ds(r0, CHUNK), :].astype(F32)
        pad_sc[pl.ds(pad + r0, CHUNK), gw + LANES:gw + 2 * LANES] = cm_ref[0, pl.ds(r0, CHUNK), :].astype(F32)
        return carry

    lax.fori_loop(0, nchunks, fill, 0)

    def conv(c, carry):
        r0 = pl.multiple_of(c * CHUNK, CHUNK)

        def one(lo, hi, w_ref, b_ref):
            win = pad_sc[pl.ds(r0, CHUNK + 2 * pad), lo:hi]
            acc = b_ref[...] + jnp.zeros((CHUNK, hi - lo), F32)
            for k in range(SSD_CONV):
                off = pad - SSD_CONV // 2 + k
                acc = acc + w_ref[k:k + 1, :] * win[off:off + CHUNK]
            return _silu(acc)

        xs_sc[pl.ds(r0, CHUNK), :] = one(0, gw, cwx_ref, cbx_ref)
        bs_sc[pl.ds(r0, CHUNK), :] = one(gw, gw + LANES, cwb_ref, cbb_ref).astype(BF16)
        cs_sc[pl.ds(r0, CHUNK), :] = one(gw + LANES, gw + 2 * LANES, cwc_ref, cbc_ref).astype(BF16)
        return carry

    lax.fori_loop(0, nchunks, conv, 0)

    lane_row = lax.broadcasted_iota(jnp.int32, (1, LANES), 1)
    dt_bias = gp_ref[0, 0:1, :]
    a_row = jnp.where(lane_row < 2 * hp, -jnp.exp(gp_ref[0, 1:2, :]), 0.0)
    d_row = gp_ref[0, 2:3, :]
    tril = tril_ref[...]
    triu = triu_ref[...]
    qi = lax.broadcasted_iota(jnp.int32, (CHUNK, CHUNK), 0)
    si = lax.broadcasted_iota(jnp.int32, (CHUNK, CHUNK), 1)
    lower = si <= qi
    eye = si == qi
    lane = lax.broadcasted_iota(jnp.int32, (CHUNK, LANES), 1)
    low_head = lane < SSD_HEAD_DIM

    def expand(v, e_ref):
        return _dot(_split3(v), e_ref[...])

    def chunk_scalars(r0):
        x = dt_ref[0, pl.ds(r0, CHUNK), :] + dt_bias
        dt = jnp.maximum(x, 0.0) + jnp.log1p(jnp.exp(-jnp.abs(x)))
        pieces = _split3(dt * a_row)
        fp = _dot(tril, pieces)
        rp = _dot(triu, pieces)
        f = fp[:, 0:LANES] + fp[:, LANES:2 * LANES] + fp[:, 2 * LANES:3 * LANES]
        r = rp[:, 0:LANES] + rp[:, LANES:2 * LANES] + rp[:, 2 * LANES:3 * LANES]
        return dt, f, r

    sf_sc[...] = jnp.zeros_like(sf_sc)
    sb_sc[...] = jnp.zeros_like(sb_sc)

    def forward(c, carry):
        r0 = pl.multiple_of(c * CHUNK, CHUNK)
        dt, f, r = chunk_scalars(r0)
        ft, rt, dtt = f.T, r.T, dt.T
        xs = xs_sc[pl.ds(r0, CHUNK), :]
        bc = bs_sc[pl.ds(r0, CHUNK), :]
        cc = cs_sc[pl.ds(r0, CHUNK), :]
        cb = _dot_nt(cc, bc)
        cbd = jnp.sum(jnp.where(eye, cb, 0.0), axis=1, keepdims=True)
        ys = []
        for pair in range(hp // 2):
            ms = []
            for i in (2 * pair, 2 * pair + 1):
                seg = jnp.where(lower, f[:, i:i + 1] - ft[i:i + 1, :], r[:, hp + i:hp + i + 1] - rt[hp + i:hp + i + 1, :])
                wdt = jnp.where(lower, dtt[i:i + 1, :], dtt[hp + i:hp + i + 1, :])
                ms.append((cb * jnp.exp(seg) * wdt).astype(BF16))
            xp = xs[:, pair * LANES:(pair + 1) * LANES]
            blockdiag = jnp.concatenate([jnp.where(low_head, xp, 0.0), jnp.where(low_head, 0.0, xp)], axis=0)
            ys.append(_dot(jnp.concatenate(ms, axis=1), blockdiag.astype(BF16)))
        y = jnp.concatenate(ys, axis=1)
        y = y + _dot(cc, sf_sc[...].astype(BF16)) * expand(jnp.exp(f), ef_ref)
        y = y + xs * expand(d_row + cbd * dt, eb_ref)
        yacc_sc[pl.ds(r0, CHUNK), :] = y
        flast = f[CHUNK - 1:CHUNK, :]
        xdec = (xs * expand(jnp.exp(flast - f) * dt, ef_ref)).astype(BF16)
        keep = expand(jnp.broadcast_to(jnp.exp(flast), (CHUNK, LANES)), ef_ref)
        sf_sc[...] = sf_sc[...] * keep + _dot_tn(bc, xdec)
        return carry

    lax.fori_loop(0, nchunks, forward, 0)

    def backward(j, carry):
        c = nchunks - 1 - j
        r0 = pl.multiple_of(c * CHUNK, CHUNK)
        dt, f, r = chunk_scalars(r0)
        xs = xs_sc[pl.ds(r0, CHUNK), :]
        bc = bs_sc[pl.ds(r0, CHUNK), :]
        cc = cs_sc[pl.ds(r0, CHUNK), :]
        y = yacc_sc[pl.ds(r0, CHUNK), :] + _dot(cc, sb_sc[...].astype(BF16)) * expand(jnp.exp(r), eb_ref)
        zc = z_ref[0, pl.ds(r0, CHUNK), :].astype(F32)
        gated = y * _silu(zc)
        ss = jnp.sum(gated * gated, axis=1, keepdims=True)
        ss_ref[0, pl.ds(r0, CHUNK), :] = jnp.broadcast_to(ss, (CHUNK, LANES))
        y_ref[0, pl.ds(r0, CHUNK), :] = (gated * ng_ref[...]).astype(BF16)
        rfirst = r[0:1, :]
        xdec = (xs * expand(jnp.exp(rfirst - r) * dt, eb_ref)).astype(BF16)
        keep = expand(jnp.broadcast_to(jnp.exp(rfirst), (CHUNK, LANES)), eb_ref)
        sb_sc[...] = sb_sc[...] * keep + _dot_tn(bc, xdec)
        return carry

    lax.fori_loop(0, nchunks, backward, 0)


def _ssd_constants():
    idx = np.arange(CHUNK)
    tril = (idx[None, :] <= idx[:, None]).astype(np.float32)
    triu = (idx[None, :] >= idx[:, None]).astype(np.float32)
    rows = np.arange(3 * LANES)[:, None] % LANES
    cols = np.arange(GROUP_WIDTH)[None, :] // SSD_HEAD_DIM
    ef = (rows == cols).astype(np.float32)
    eb = (rows == cols + HEADS_PER_GROUP).astype(np.float32)
    return tuple(jnp.asarray(a, dtype=BF16) for a in (tril, triu, ef, eb))


def _ssd(xbc, z, dt, conv_w, conv_b, gp, norm_g):
    b, seq, _ = xbc.shape
    assert seq % CHUNK == 0
    gw = GROUP_WIDTH
    xblk = gw // LANES
    nx = SSD_WIDTH // LANES
    consts = _ssd_constants()
    seq_blk = lambda w, f: pl.BlockSpec((1, seq, w), lambda bi, g: (bi, 0, f(g)))
    par = lambda rows, w, f: pl.BlockSpec((rows, w), lambda bi, g: (0, f(g)))
    full = lambda a: pl.BlockSpec(a.shape, lambda bi, g: (0,) * a.ndim)
    return pl.pallas_call(
        _ssd_kernel,
        grid=(b, SSD_GROUPS),
        in_specs=[seq_blk(gw, lambda g: g), seq_blk(LANES, lambda g: nx + g), seq_blk(LANES, lambda g: nx + SSD_GROUPS + g),
                  seq_blk(gw, lambda g: g), seq_blk(LANES, lambda g: g),
                  par(SSD_CONV, gw, lambda g: g), par(SSD_CONV, LANES, lambda g: nx + g),
                  par(SSD_CONV, LANES, lambda g: nx + SSD_GROUPS + g),
                  par(1, gw, lambda g: g), par(1, LANES, lambda g: nx + g), par(1, LANES, lambda g: nx + SSD_GROUPS + g),
                  pl.BlockSpec((1, 8, LANES), lambda bi, g: (g, 0, 0)), par(1, gw, lambda g: g)]
                 + [full(a) for a in consts],
        out_specs=[seq_blk(gw, lambda g: g), seq_blk(LANES, lambda g: g)],
        out_shape=[jax.ShapeDtypeStruct((b, seq, SSD_WIDTH), BF16), jax.ShapeDtypeStruct((b, seq, SSD_GROUPS * LANES), F32)],
        scratch_shapes=[pltpu.VMEM((seq + 16, gw + 2 * LANES), F32), pltpu.VMEM((seq, gw), F32),
                        pltpu.VMEM((seq, LANES), BF16), pltpu.VMEM((seq, LANES), BF16), pltpu.VMEM((seq, gw), F32),
                        pltpu.VMEM((SSD_STATE, gw), F32), pltpu.VMEM((SSD_STATE, gw), F32)],
        compiler_params=pltpu.CompilerParams(dimension_semantics=("arbitrary", "arbitrary"),
                                             vmem_limit_bytes=VMEM_LIMIT),
        name="bidir_ssd",
    )(xbc, xbc, xbc, z, dt, conv_w, conv_w, conv_w, conv_b, conv_b, conv_b, gp, norm_g, *consts)


def _outproj_kernel(attn_ref, y_ref, ss_ref, h_ref, wa_ref, ws_ref, g_ref, b_ref, o_ref):
    ss = ss_ref[...]
    total = ss[:, 0:1]
    for g in range(1, SSD_GROUPS):
        total = total + ss[:, g * LANES:g * LANES + 1]
    rs = lax.rsqrt(total * (1.0 / SSD_WIDTH) + EPS)
    mix = _dot(attn_ref[...], wa_ref[...]) + rs * _dot(y_ref[...], ws_ref[...])
    o_ref[...] = _layer_norm(DEEPNORM_ALPHA * h_ref[...] + mix, g_ref[...], b_ref[...])


def _outproj(attn, y, ss, h, wa, ws, g, b):
    t = h.shape[0]
    tm = TOKEN_TILE
    row = lambda w: pl.BlockSpec((tm, w), lambda i: (i, 0))
    full = lambda a: pl.BlockSpec(a.shape, lambda i: (0,) * a.ndim)
    return pl.pallas_call(
        _outproj_kernel,
        grid=(t // tm,),
        in_specs=[row(ATTN_WIDTH), row(SSD_WIDTH), row(SSD_GROUPS * LANES), row(D_MODEL), full(wa), full(ws), full(g), full(b)],
        out_specs=row(D_MODEL),
        out_shape=jax.ShapeDtypeStruct((t, D_MODEL), F32),
        compiler_params=pltpu.CompilerParams(dimension_semantics=("arbitrary",), vmem_limit_bytes=VMEM_LIMIT),
        name="out_proj_ln",
    )(attn, y, ss, h, wa, ws, g, b)


def _ffn_kernel(tiles_per_seq, hp_ref, h_ref, hn_ref, wup_ref, cw_ref, cb_ref, wdn_ref, g_ref, b_ref, o_ref,
                hb_sc, acc_sc):
    tm = h_ref.shape[0]
    pos = pl.program_id(0) % tiles_per_seq
    h = h_ref[...]
    hb_sc[0:FF_HALO, :] = jnp.where(pos == 0, 0.0, hp_ref[...]).astype(BF16)
    hb_sc[FF_HALO:FF_HALO + tm, :] = h.astype(BF16)
    hb_sc[FF_HALO + tm:, :] = jnp.where(pos == tiles_per_seq - 1, 0.0, hn_ref[...]).astype(BF16)
    for j in range(D_FF // FF_TILE):
        cols = slice(j * FF_TILE, (j + 1) * FF_TILE)
        vcols = slice(D_FF + j * FF_TILE, D_FF + (j + 1) * FF_TILE)
        g_ext = _dot(hb_sc[...], wup_ref[:, cols])
        val = _dot(hb_sc[FF_HALO:FF_HALO + tm, :], wup_ref[:, vcols])
        gate = cb_ref[:, cols] + cw_ref[1:2, cols] * g_ext[FF_HALO:FF_HALO + tm]
        gate = gate + cw_ref[0:1, cols] * g_ext[FF_HALO - 1:FF_HALO - 1 + tm]
        gate = gate + cw_ref[2:3, cols] * g_ext[FF_HALO + 1:FF_HALO + 1 + tm]
        part = _dot((_silu(gate) * val).astype(BF16), wdn_ref[cols, :])
        if j == 0:
            acc_sc[...] = part
        else:
            acc_sc[...] += part
    o_ref[...] = _layer_norm(DEEPNORM_ALPHA * h + acc_sc[...], g_ref[...], b_ref[...])


def _ffn(h, seq, wup, cw, cb, wdn, g, b):
    t = h.shape[0]
    tm = TOKEN_TILE
    assert seq % tm == 0 and tm % FF_HALO == 0 and D_FF % FF_TILE == 0
    tiles_per_seq = seq // tm
    hb = tm // FF_HALO
    nhb = t // FF_HALO
    row = pl.BlockSpec((tm, D_MODEL), lambda i: (i, 0))
    prev = pl.BlockSpec((FF_HALO, D_MODEL), lambda i: (jnp.maximum(i * hb - 1, 0), 0))
    nxt = pl.BlockSpec((FF_HALO, D_MODEL), lambda i: (jnp.minimum((i + 1) * hb, nhb - 1), 0))
    full = lambda a: pl.BlockSpec(a.shape, lambda i: (0,) * a.ndim, pipeline_mode=pl.Buffered(1))
    return pl.pallas_call(
        functools.partial(_ffn_kernel, tiles_per_seq),
        grid=(t // tm,),
        in_specs=[prev, row, nxt, full(wup), full(cw), full(cb), full(wdn), full(g), full(b)],
        out_specs=row,
        out_shape=jax.ShapeDtypeStruct((t, D_MODEL), F32),
        scratch_shapes=[pltpu.VMEM((tm + 2 * FF_HALO, D_MODEL), BF16), pltpu.VMEM((tm, D_MODEL), F32)],
        compiler_params=pltpu.CompilerParams(dimension_semantics=("arbitrary",), vmem_limit_bytes=VMEM_LIMIT),
        name="ffn_ln",
    )(h, h, h, wup, cw, cb, wdn, g, b)


def _rope_tables(seq):
    half = HEAD_DIM // 2
    inv_freq = ROPE_THETA ** (-jnp.arange(0, HEAD_DIM, 2, dtype=F32) / HEAD_DIM)
    ang = jnp.arange(seq, dtype=F32)[:, None] * inv_freq[None, :]
    cos, sin = jnp.cos(ang), jnp.sin(ang)
    reps = LANES // HEAD_DIM
    return jnp.tile(jnp.concatenate([cos, cos], axis=1), (1, reps)), jnp.tile(jnp.concatenate([-sin, sin], axis=1), (1, reps))


def _per_group(fwd, bwd, offset_f, offset_b):
    out = jnp.zeros((SSD_GROUPS, LANES), F32)
    for g in range(SSD_GROUPS):
        sl = slice(g * HEADS_PER_GROUP, (g + 1) * HEADS_PER_GROUP)
        if fwd is not None:
            out = out.at[g, offset_f:offset_f + HEADS_PER_GROUP].set(fwd[sl])
        if bwd is not None:
            out = out.at[g, offset_b:offset_b + HEADS_PER_GROUP].set(bwd[sl])
    return out


def _layer_params(w_in, ssd_a_log, ssd_dt_bias, ssd_d, w_out, w_up, w_down):
    o_k = ATTN_WIDTH + KV_WIDTH
    o_z = o_k + KV_WIDTH
    o_x = o_z + SSD_WIDTH
    o_dt = o_x + XBC_WIDTH
    hp = HEADS_PER_GROUP
    wdt_src = w_in[:, o_dt:]
    wdt = jnp.zeros((D_MODEL, SSD_GROUPS * LANES), F32)
    for g in range(SSD_GROUPS):
        wdt = wdt.at[:, g * LANES:g * LANES + hp].set(wdt_src[:, g * hp:(g + 1) * hp])
        wdt = wdt.at[:, g * LANES + hp:g * LANES + 2 * hp].set(wdt_src[:, SSD_HEADS + g * hp:SSD_HEADS + (g + 1) * hp])
    gp = jnp.zeros((SSD_GROUPS, 8, LANES), F32)
    gp = gp.at[:, 0, :].set(_per_group(ssd_dt_bias[0], ssd_dt_bias[1], 0, hp))
    gp = gp.at[:, 1, :].set(_per_group(ssd_a_log[0], ssd_a_log[1], 0, hp))
    gp = gp.at[:, 2, :].set(_per_group(None, ssd_d, 0, hp))
    return dict(
        wqk=w_in[:, :o_k].astype(BF16), wv=w_in[:, o_k:o_z].astype(BF16), wz=w_in[:, o_z:o_x].astype(BF16),
        wx=w_in[:, o_x:o_dt].astype(BF16), wdt=wdt.astype(BF16), gp=gp,
        wa=w_out[:ATTN_WIDTH].astype(BF16), ws=w_out[ATTN_WIDTH:].astype(BF16),
        wup=w_up.astype(BF16), wdn=w_down.astype(BF16))


def _trunk(x, ln_in_g, ln_in_b, layers):
    b, seq, _ = x.shape
    cos_t, sin_t = _rope_tables(seq)
    h = x.reshape(b * seq, D_MODEL)
    lng, lnb = ln_in_g[None, :], ln_in_b[None, :]
    for li, p in enumerate(layers):
        outs = _inproj(h, seq, li == 0, lng, lnb, p["wqk"], p["wv"], p["wz"], p["wx"], p["wdt"], cos_t, sin_t)
        if li == 0:
            h, outs = outs[0], outs[1:]
        q, kv, z, xbc, dt = outs
        three = lambda a: a.reshape(b, seq, a.shape[-1])
        attn = _attention(three(q), three(kv), p["sink"])
        y, ss = _ssd(three(xbc), three(z), three(dt), p["conv_w"], p["conv_b"], p["gp"], p["norm_g"])
        h = _outproj(attn.reshape(b * seq, -1), y.reshape(b * seq, -1), ss.reshape(b * seq, -1), h,
                     p["wa"], p["ws"], p["ln1_g"], p["ln1_b"])
        h = _ffn(h, seq, p["wup"], p["ffn_cw"], p["ffn_cb"], p["wdn"], p["ln2_g"], p["ln2_b"])
    return h.reshape(b, seq, D_MODEL)


def kernel(x_prompt, x_sample, ln_in_g, ln_in_b, w_in, attn_sink, ssd_conv_w, ssd_conv_b, ssd_a_log, ssd_dt_bias, ssd_d, ssd_norm_g, w_out, ln1_g, ln1_b, w_up, ffn_conv_w, ffn_conv_b, w_down, ln2_g, ln2_b):
    layers = []
    for i in range(w_in.shape[0]):
        p = _layer_params(w_in[i], ssd_a_log[i], ssd_dt_bias[i], ssd_d[i], w_out[i], w_up[i], w_down[i])
        p.update(sink=attn_sink[i], conv_w=ssd_conv_w[i], conv_b=ssd_conv_b[i][None, :], norm_g=ssd_norm_g[i][None, :],
                 ln1_g=ln1_g[i][None, :], ln1_b=ln1_b[i][None, :], ffn_cw=ffn_conv_w[i], ffn_cb=ffn_conv_b[i][None, :],
                 ln2_g=ln2_g[i][None, :], ln2_b=ln2_b[i][None, :])
        layers.append(p)
    return (_trunk(x_prompt, ln_in_g, ln_in_b, layers), _trunk(x_sample, ln_in_g, ln_in_b, layers))
```

```python
import functools
import math

import jax
import jax.numpy as jnp
import numpy as np
from jax import lax
from jax.experimental import pallas as pl
from jax.experimental.pallas import tpu as pltpu

F32 = jnp.float32
BF16 = jnp.bfloat16

D_MODEL = 1024
DEPTH = 4
HEAD_DIM = 64
N_Q_HEADS = 8
N_KV_HEADS = 2
ATTN_WIDTH = N_Q_HEADS * HEAD_DIM
KV_WIDTH = N_KV_HEADS * HEAD_DIM
WINDOW = 128
ROPE_THETA = 10000.0
SSD_HEADS = 16
SSD_HEAD_DIM = 64
SSD_WIDTH = SSD_HEADS * SSD_HEAD_DIM
SSD_GROUPS = 2
HEADS_PER_GROUP = SSD_HEADS // SSD_GROUPS
GROUP_WIDTH = HEADS_PER_GROUP * SSD_HEAD_DIM
SSD_STATE = 128
SSD_CONV = 5
CHUNK = 128
XBC_WIDTH = SSD_WIDTH + 2 * SSD_GROUPS * SSD_STATE
D_FF = 2816
FFN_CONV = 3
DEEPNORM_ALPHA = (2 * DEPTH) ** 0.25
EPS = 1e-5
MASK_VALUE = -1e30

LANES = 128
VMEM_LIMIT = 56 * 1024 * 1024
TOKEN_TILE = 512
FFN_TOKEN_TILE = 1024
ATTN_BQ = 256
FF_TILE = 256
FF_HALO = 16
CONV_PAD = 16


def _dot(a, b):
    return jnp.dot(a, b, preferred_element_type=F32)


def _dot_nt(a, b):
    return lax.dot_general(a, b, (((1,), (1,)), ((), ())), preferred_element_type=F32)


def _layer_norm(x, g, b):
    mu = jnp.mean(x, axis=-1, keepdims=True)
    xc = x - mu
    var = jnp.mean(xc * xc, axis=-1, keepdims=True)
    return xc * lax.rsqrt(var + EPS) * g + b


def _silu(x):
    return x * (1.0 / (1.0 + jnp.exp(-x)))


def _split3(x):
    hi = x.astype(BF16)
    r1 = x - hi.astype(F32)
    mid = r1.astype(BF16)
    lo = (r1 - mid.astype(F32)).astype(BF16)
    return jnp.concatenate([hi, mid, lo], axis=1)


def _inproj_kernel(apply_ln, x_ref, lng_ref, lnb_ref, wqk_ref, wv_ref, wz_ref, wx_ref, wdt_ref,
                   cos_ref, sin_ref, *outs):
    if apply_ln:
        h_ref, q_ref, kv_ref, z_ref, xbc_ref, dt_ref = outs
    else:
        q_ref, kv_ref, z_ref, xbc_ref, dt_ref = outs
    x = x_ref[...]
    if apply_ln:
        x = _layer_norm(x, lng_ref[...], lnb_ref[...])
        h_ref[...] = x
    xb = x.astype(BF16)
    tm = x.shape[0]
    cos = cos_ref[...]
    sin = sin_ref[...]
    lane = lax.broadcasted_iota(jnp.int32, (tm, LANES), 1)
    first_half = (lane % HEAD_DIM) < (HEAD_DIM // 2)
    low_head = lane < HEAD_DIM

    def rope(t):
        swapped = jnp.where(first_half, pltpu.roll(t, LANES - HEAD_DIM // 2, 1), pltpu.roll(t, HEAD_DIM // 2, 1))
        return t * cos + swapped * sin

    def both_halves(t):
        sw = pltpu.roll(t, HEAD_DIM, 1)
        return jnp.where(low_head, t, sw), jnp.where(low_head, sw, t)

    qk = _dot(xb, wqk_ref[...])
    scale = HEAD_DIM ** -0.5
    for j in range(ATTN_WIDTH // LANES):
        q_ref[:, j * LANES:(j + 1) * LANES] = (rope(qk[:, j * LANES:(j + 1) * LANES]) * scale).astype(BF16)
    k0, k1 = both_halves(rope(qk[:, ATTN_WIDTH:ATTN_WIDTH + KV_WIDTH]))
    v0, v1 = both_halves(_dot(xb, wv_ref[...]))
    kv_ref[:, 0 * LANES:1 * LANES] = k0.astype(BF16)
    kv_ref[:, 1 * LANES:2 * LANES] = k1.astype(BF16)
    kv_ref[:, 2 * LANES:3 * LANES] = v0.astype(BF16)
    kv_ref[:, 3 * LANES:4 * LANES] = v1.astype(BF16)
    z_ref[...] = _dot(xb, wz_ref[...]).astype(BF16)
    xbc_ref[...] = _dot(xb, wx_ref[...]).astype(BF16)
    dt_ref[...] = _dot(xb, wdt_ref[...])


def _inproj(x2d, seq, apply_ln, lng, lnb, wqk, wv, wz, wx, wdt, cos_t, sin_t):
    t = x2d.shape[0]
    tm = TOKEN_TILE
    assert t % tm == 0 and seq % tm == 0
    tiles_per_seq = seq // tm
    row = lambda w: pl.BlockSpec((tm, w), lambda i: (i, 0))
    full = lambda a: pl.BlockSpec(a.shape, lambda i: (0,) * a.ndim)
    tab = pl.BlockSpec((tm, LANES), lambda i: (i % tiles_per_seq, 0))
    out_shape = [jax.ShapeDtypeStruct((t, ATTN_WIDTH), BF16), jax.ShapeDtypeStruct((t, 4 * LANES), BF16),
                 jax.ShapeDtypeStruct((t, SSD_WIDTH), BF16), jax.ShapeDtypeStruct((t, XBC_WIDTH), BF16),
                 jax.ShapeDtypeStruct((t, SSD_GROUPS * LANES), F32)]
    out_specs = [row(ATTN_WIDTH), row(4 * LANES), row(SSD_WIDTH), row(XBC_WIDTH), row(SSD_GROUPS * LANES)]
    if apply_ln:
        out_shape = [jax.ShapeDtypeStruct((t, D_MODEL), F32)] + out_shape
        out_specs = [row(D_MODEL)] + out_specs
    return pl.pallas_call(
        functools.partial(_inproj_kernel, apply_ln),
        grid=(t // tm,),
        in_specs=[row(D_MODEL), full(lng), full(lnb), full(wqk), full(wv), full(wz), full(wx), full(wdt), tab, tab],
        out_specs=out_specs,
        out_shape=out_shape,
        compiler_params=pltpu.CompilerParams(dimension_semantics=("arbitrary",), vmem_limit_bytes=VMEM_LIMIT),
        name="in_proj_ln" if apply_ln else "in_proj",
    )(x2d, lng, lnb, wqk, wv, wz, wx, wdt, cos_t, sin_t)


def _attn_kernel(seq, sink_ref, q_ref, kl_ref, km_ref, kr_ref, vl_ref, vm_ref, vr_ref, o_ref):
    bq = q_ref.shape[1]
    ks = bq + 2 * WINDOW
    start = pl.program_id(1) * bq
    qi = lax.broadcasted_iota(jnp.int32, (bq, ks), 0)
    kc = lax.broadcasted_iota(jnp.int32, (bq, ks), 1) - WINDOW
    d = kc - qi
    kpos = kc + start
    allow = (jnp.abs(d) <= WINDOW) & (kpos >= 0) & (kpos < seq)
    lane = lax.broadcasted_iota(jnp.int32, (bq, LANES), 1)
    low_head = lane < HEAD_DIM
    group = N_Q_HEADS // N_KV_HEADS
    for h in range(N_KV_HEADS):
        ksl = slice(h * LANES, (h + 1) * LANES)
        kk = jnp.concatenate([kl_ref[0, :, ksl], km_ref[0, :, ksl], kr_ref[0, :, ksl]], axis=0)
        vv = jnp.concatenate([vl_ref[0, :, ksl], vm_ref[0, :, ksl], vr_ref[0, :, ksl]], axis=0)
        qs = []
        for r in range(group):
            head = group * h + r
            qt = q_ref[0, :, (head // 2) * LANES:(head // 2 + 1) * LANES]
            keep = low_head if head % 2 == 0 else jnp.logical_not(low_head)
            qs.append(jnp.where(keep, qt, jnp.zeros_like(qt)))
        sc = _dot_nt(jnp.concatenate(qs, axis=0), kk)
        ps, inv = [], []
        for r in range(group):
            sink = sink_ref[group * h + r]
            s = jnp.where(allow, sc[r * bq:(r + 1) * bq], MASK_VALUE)
            m = jnp.maximum(jnp.max(s, axis=-1, keepdims=True), sink)
            p = jnp.exp(s - m)
            denom = jnp.sum(p, axis=-1, keepdims=True) + jnp.exp(sink - m)
            ps.append(p.astype(BF16))
            inv.append(1.0 / denom)
        pv = _dot(jnp.concatenate(ps, axis=0), vv)
        for pair in range(group // 2):
            even = pv[(2 * pair) * bq:(2 * pair + 1) * bq] * inv[2 * pair]
            odd = pv[(2 * pair + 1) * bq:(2 * pair + 2) * bq] * inv[2 * pair + 1]
            col = (group * h) // 2 + pair
            o_ref[0, :, col * LANES:(col + 1) * LANES] = jnp.where(low_head, even, odd).astype(BF16)


def _attention(q, kv, sink):
    b, seq, _ = q.shape
    bq = ATTN_BQ
    assert seq % bq == 0 and bq % WINDOW == 0
    r = bq // WINDOW
    nwb = seq // WINDOW
    side = lambda lane_blk, f: pl.BlockSpec((1, WINDOW, 2 * LANES), lambda bi, i: (bi, f(i), lane_blk))
    mid = lambda lane_blk: pl.BlockSpec((1, bq, 2 * LANES), lambda bi, i: (bi, i, lane_blk))
    left = lambda i: jnp.maximum(i * r - 1, 0)
    right = lambda i: jnp.minimum((i + 1) * r, nwb - 1)
    return pl.pallas_call(
        functools.partial(_attn_kernel, seq),
        grid=(b, seq // bq),
        in_specs=[pl.BlockSpec(memory_space=pltpu.SMEM),
                  pl.BlockSpec((1, bq, ATTN_WIDTH), lambda bi, i: (bi, i, 0)),
                  side(0, left), mid(0), side(0, right), side(1, left), mid(1), side(1, right)],
        out_specs=pl.BlockSpec((1, bq, ATTN_WIDTH), lambda bi, i: (bi, i, 0)),
        out_shape=jax.ShapeDtypeStruct((b, seq, ATTN_WIDTH), BF16),
        compiler_params=pltpu.CompilerParams(dimension_semantics=("arbitrary", "arbitrary"),
                                             vmem_limit_bytes=VMEM_LIMIT),
        name="window_attn",
    )(sink, q, kv, kv, kv, kv, kv, kv)


def _ssd_kernel(x_ref, bm_ref, cm_ref, z_ref, dt_ref, cwx_ref, cwb_ref, cwc_ref, cbx_ref, cbb_ref, cbc_ref,
                gp_ref, dexp_ref, ng_ref, tril_ref, triu_ref, ef_ref, eb_ref, sh_ref, y_ref, ss_ref,
                pad_sc, xs_sc, bs_sc, cs_sc, bt_sc, yacc_sc, locb_sc, er_sc, keepb_sc, sf_sc, sb_sc):
    seq = x_ref.shape[1]
    nchunks = seq // CHUNK
    gw = GROUP_WIDTH
    hp = HEADS_PER_GROUP
    pad = CONV_PAD
    taps = [k for k in range(SSD_CONV) if k != SSD_CONV // 2]

    zeros = jnp.zeros((pad, gw + 2 * LANES), BF16)
    pad_sc[0:pad, :] = zeros
    pad_sc[pad + seq:pad + seq + pad, :] = zeros

    def fill(c, carry):
        r0 = pl.multiple_of(c * CHUNK, CHUNK)
        pad_sc[pl.ds(pad + r0, CHUNK), 0:gw] = x_ref[0, pl.ds(r0, CHUNK), :]
        pad_sc[pl.ds(pad + r0, CHUNK), gw:gw + LANES] = bm_ref[0, pl.ds(r0, CHUNK), :]
        pad_sc[pl.ds(pad + r0, CHUNK), gw + LANES:gw + 2 * LANES] = cm_ref[0, pl.ds(r0, CHUNK), :]
        return carry

    lax.fori_loop(0, nchunks, fill, 0)

    def conv(c, carry):
        r0 = pl.multiple_of(c * CHUNK, CHUNK)
        shifted = _dot(sh_ref[...], pad_sc[pl.ds(r0, CHUNK + 2 * pad), :])
        centre = pad_sc[pl.ds(r0 + pad, CHUNK), :].astype(F32)

        def one(lo, hi, w_ref, b_ref):
            acc = b_ref[...] + w_ref[SSD_CONV // 2:SSD_CONV // 2 + 1, :] * centre[:, lo:hi]
            for j, k in enumerate(taps):
                acc = acc + w_ref[k:k + 1, :] * shifted[j * CHUNK:(j + 1) * CHUNK, lo:hi]
            return _silu(acc)

        xs_sc[pl.ds(r0, CHUNK), :] = one(0, gw, cwx_ref, cbx_ref)
        bconv = one(gw, gw + LANES, cwb_ref, cbb_ref)
        bs_sc[pl.ds(r0, CHUNK), :] = bconv.astype(BF16)
        bt_sc[c] = bconv.T
        cs_sc[pl.ds(r0, CHUNK), :] = one(gw + LANES, gw + 2 * LANES, cwc_ref, cbc_ref).astype(BF16)
        return carry

    lax.fori_loop(0, nchunks, conv, 0, unroll=2)

    lane_row = lax.broadcasted_iota(jnp.int32, (1, LANES), 1)
    dt_bias = gp_ref[0, 0:1, :]
    a_row = jnp.where(lane_row < 2 * hp, -jnp.exp(gp_ref[0, 1:2, :]), 0.0)
    dexp = dexp_ref[...]
    tril = tril_ref[...]
    triu = triu_ref[...]
    qi = lax.broadcasted_iota(jnp.int32, (CHUNK, CHUNK), 0)
    si = lax.broadcasted_iota(jnp.int32, (CHUNK, CHUNK), 1)
    lower = si <= qi
    before = si < qi
    after = si > qi
    low_head = si < SSD_HEAD_DIM
    fwd_lane = si < hp

    def blockdiag(t):
        return jnp.concatenate([jnp.where(low_head, t, 0.0), jnp.where(low_head, 0.0, t)], axis=0).astype(BF16)

    def head_rows(v, e_ref):
        return _dot(_split3(jnp.broadcast_to(v, (8, LANES))), e_ref[...])

    def chunk_scalars(r0):
        x = dt_ref[0, pl.ds(r0, CHUNK), :] + dt_bias
        dt = jnp.maximum(x, 0.0) + jnp.log1p(jnp.exp(-jnp.abs(x)))
        pieces = _split3(dt * a_row)
        fp = _dot(tril, pieces)
        rp = _dot(triu, pieces)
        f = fp[:, 0:LANES] + fp[:, LANES:2 * LANES] + fp[:, 2 * LANES:3 * LANES]
        r = rp[:, 0:LANES] + rp[:, LANES:2 * LANES] + rp[:, 2 * LANES:3 * LANES]
        return dt, f, r

    sf_sc[...] = jnp.zeros_like(sf_sc)
    sb_sc[...] = jnp.zeros_like(sb_sc)

    def expand(v, e_ref):
        return _dot(_split3(v), e_ref[...])

    def ascend(c, carry):
        r0 = pl.multiple_of(c * CHUNK, CHUNK)
        dt, f, r = chunk_scalars(r0)
        fr = jnp.where(fwd_lane, f, r)
        pt = jnp.where(si < 2 * hp, fr, dt).T
        cums_t = pt[0:2 * hp, :]
        dt_t = pt[2 * hp:4 * hp, :]
        dte_t = jnp.exp(jnp.where(qi[0:2 * hp] < hp, cums_t[:, CHUNK - 1:CHUNK] - cums_t, cums_t[:, 0:1] - cums_t)) * dt_t
        keep_f = head_rows(jnp.exp(f[CHUNK - 1:CHUNK, :]), ef_ref)
        keepb_sc[c] = head_rows(jnp.exp(r[0:1, :]), eb_ref)
        er_sc[c] = expand(jnp.exp(r), eb_ref)
        xs = xs_sc[pl.ds(r0, CHUNK), :]
        cc = cs_sc[pl.ds(r0, CHUNK), :]
        cb = _dot_nt(cc, bs_sc[pl.ds(r0, CHUNK), :])
        yoff = _dot(cc, sf_sc[...].astype(BF16)) * expand(jnp.exp(f), ef_ref)
        bt = bt_sc[c]
        for pair in range(hp // 2):
            ms, bfs, bbs = [], [], []
            for i in (2 * pair, 2 * pair + 1):
                seg = jnp.where(lower, fr[:, i:i + 1] - cums_t[i:i + 1, :], fr[:, hp + i:hp + i + 1] - cums_t[hp + i:hp + i + 1, :])
                dtf_row = dt_t[i:i + 1, :]
                dtb_row = dt_t[hp + i:hp + i + 1, :]
                w = jnp.where(before, dtf_row, jnp.where(after, dtb_row, dtf_row + dtb_row))
                ms.append((cb * jnp.exp(seg) * w).astype(BF16))
                bfs.append((bt * dte_t[i:i + 1, :]).astype(BF16))
                bbs.append((bt * dte_t[hp + i:hp + i + 1, :]).astype(BF16))
            sl = slice(pair * LANES, (pair + 1) * LANES)
            lhs = jnp.concatenate([jnp.concatenate(ms, axis=1), jnp.concatenate(bfs, axis=1),
                                   jnp.concatenate(bbs, axis=1)], axis=0)
            res = _dot(lhs, blockdiag(xs[:, sl]))
            yacc_sc[pl.ds(r0, CHUNK), sl] = res[0:CHUNK] + yoff[:, sl] + xs[:, sl] * dexp[:, sl]
            sf_sc[:, sl] = sf_sc[:, sl] * keep_f[0:1, sl] + res[CHUNK:2 * CHUNK]
            locb_sc[c, :, sl] = res[2 * CHUNK:3 * CHUNK]
        return carry

    lax.fori_loop(0, nchunks, ascend, 0, unroll=2)

    def descend(j, carry):
        c = nchunks - 1 - j
        r0 = pl.multiple_of(c * CHUNK, CHUNK)
        sb = sb_sc[...]
        y = yacc_sc[pl.ds(r0, CHUNK), :] + _dot(cs_sc[pl.ds(r0, CHUNK), :], sb.astype(BF16)) * er_sc[c]
        sb_sc[...] = sb * keepb_sc[c][0:1, :] + locb_sc[c]
        gated = y * _silu(z_ref[0, pl.ds(r0, CHUNK), :].astype(F32))
        ss = jnp.sum(gated * gated, axis=1, keepdims=True)
        ss_ref[0, pl.ds(r0, CHUNK), :] = jnp.broadcast_to(ss, (CHUNK, LANES))
        y_ref[0, pl.ds(r0, CHUNK), :] = (gated * ng_ref[...]).astype(BF16)
        return carry

    lax.fori_loop(0, nchunks, descend, 0, unroll=2)


def _ssd_constants():
    idx = np.arange(CHUNK)
    tril = (idx[None, :] <= idx[:, None]).astype(np.float32)
    triu = (idx[None, :] >= idx[:, None]).astype(np.float32)
    rows = np.arange(3 * LANES)[:, None] % LANES
    cols = np.arange(GROUP_WIDTH)[None, :] // SSD_HEAD_DIM
    ef = (rows == cols).astype(np.float32)
    eb = (rows == cols + HEADS_PER_GROUP).astype(np.float32)
    taps = [k for k in range(SSD_CONV) if k != SSD_CONV // 2]
    src = np.arange(CHUNK + 2 * CONV_PAD)[None, :]
    shift = np.concatenate([(src == idx[:, None] + CONV_PAD + k - SSD_CONV // 2) for k in taps], axis=0).astype(np.float32)
    return tuple(jnp.asarray(a, dtype=BF16) for a in (tril, triu, ef, eb, shift))


def _ssd(xbc, z, dt, conv_w, conv_b, gp, d_exp, norm_g):
    b, seq, _ = xbc.shape
    assert seq % CHUNK == 0
    gw = GROUP_WIDTH
    nx = SSD_WIDTH // LANES
    nchunks = seq // CHUNK
    consts = _ssd_constants()
    seq_blk = lambda w, f: pl.BlockSpec((1, seq, w), lambda bi, g: (bi, 0, f(g)))
    par = lambda rows, w, f: pl.BlockSpec((rows, w), lambda bi, g: (0, f(g)))
    full = lambda a: pl.BlockSpec(a.shape, lambda bi, g: (0,) * a.ndim)
    return pl.pallas_call(
        _ssd_kernel,
        grid=(b, SSD_GROUPS),
        in_specs=[seq_blk(gw, lambda g: g), seq_blk(LANES, lambda g: nx + g), seq_blk(LANES, lambda g: nx + SSD_GROUPS + g),
                  seq_blk(gw, lambda g: g), seq_blk(LANES, lambda g: g),
                  par(SSD_CONV, gw, lambda g: g), par(SSD_CONV, LANES, lambda g: nx + g),
                  par(SSD_CONV, LANES, lambda g: nx + SSD_GROUPS + g),
                  par(1, gw, lambda g: g), par(1, LANES, lambda g: nx + g), par(1, LANES, lambda g: nx + SSD_GROUPS + g),
                  pl.BlockSpec((1, 8, LANES), lambda bi, g: (g, 0, 0)), par(1, gw, lambda g: g), par(1, gw, lambda g: g)]
                 + [full(a) for a in consts],
        out_specs=[seq_blk(gw, lambda g: g), seq_blk(LANES, lambda g: g)],
        out_shape=[jax.ShapeDtypeStruct((b, seq, SSD_WIDTH), BF16), jax.ShapeDtypeStruct((b, seq, SSD_GROUPS * LANES), F32)],
        scratch_shapes=[pltpu.VMEM((seq + 2 * CONV_PAD, gw + 2 * LANES), BF16), pltpu.VMEM((seq, gw), F32),
                        pltpu.VMEM((seq, LANES), BF16), pltpu.VMEM((seq, LANES), BF16),
                        pltpu.VMEM((nchunks, SSD_STATE, CHUNK), F32), pltpu.VMEM((seq, gw), F32),
                        pltpu.VMEM((nchunks, SSD_STATE, gw), F32), pltpu.VMEM((nchunks, CHUNK, gw), F32),
                        pltpu.VMEM((nchunks, 8, gw), F32),
                        pltpu.VMEM((SSD_STATE, gw), F32), pltpu.VMEM((SSD_STATE, gw), F32)],
        compiler_params=pltpu.CompilerParams(dimension_semantics=("arbitrary", "arbitrary"),
                                             vmem_limit_bytes=VMEM_LIMIT),
        name="bidir_ssd",
    )(xbc, xbc, xbc, z, dt, conv_w, conv_w, conv_w, conv_b, conv_b, conv_b, gp, d_exp, norm_g, *consts)


def _outproj_kernel(attn_ref, y_ref, ss_ref, h_ref, wa_ref, ws_ref, g_ref, b_ref, o_ref):
    ss = ss_ref[...]
    total = ss[:, 0:1]
    for g in range(1, SSD_GROUPS):
        total = total + ss[:, g * LANES:g * LANES + 1]
    rs = lax.rsqrt(total * (1.0 / SSD_WIDTH) + EPS)
    mix = _dot(attn_ref[...], wa_ref[...]) + rs * _dot(y_ref[...], ws_ref[...])
    o_ref[...] = _layer_norm(DEEPNORM_ALPHA * h_ref[...] + mix, g_ref[...], b_ref[...])


def _outproj(attn, y, ss, h, wa, ws, g, b):
    t = h.shape[0]
    tm = TOKEN_TILE
    row = lambda w: pl.BlockSpec((tm, w), lambda i: (i, 0))
    full = lambda a: pl.BlockSpec(a.shape, lambda i: (0,) * a.ndim)
    return pl.pallas_call(
        _outproj_kernel,
        grid=(t // tm,),
        in_specs=[row(ATTN_WIDTH), row(SSD_WIDTH), row(SSD_GROUPS * LANES), row(D_MODEL), full(wa), full(ws), full(g), full(b)],
        out_specs=row(D_MODEL),
        out_shape=jax.ShapeDtypeStruct((t, D_MODEL), F32),
        compiler_params=pltpu.CompilerParams(dimension_semantics=("arbitrary",), vmem_limit_bytes=VMEM_LIMIT),
        name="out_proj_ln",
    )(attn, y, ss, h, wa, ws, g, b)


def _ffn_kernel(tiles_per_seq, hp_ref, h_ref, hn_ref, wup_ref, cw_ref, cb_ref, wdn_ref, g_ref, b_ref, o_ref,
                hb_sc, acc_sc):
    tm = h_ref.shape[0]
    pos = pl.program_id(0) % tiles_per_seq
    h = h_ref[...]
    hb_sc[0:FF_HALO, :] = jnp.where(pos == 0, 0.0, hp_ref[...]).astype(BF16)
    hb_sc[FF_HALO:FF_HALO + tm, :] = h.astype(BF16)
    hb_sc[FF_HALO + tm:, :] = jnp.where(pos == tiles_per_seq - 1, 0.0, hn_ref[...]).astype(BF16)
    for j in range(D_FF // FF_TILE):
        cols = slice(j * FF_TILE, (j + 1) * FF_TILE)
        vcols = slice(D_FF + j * FF_TILE, D_FF + (j + 1) * FF_TILE)
        g_ext = _dot(hb_sc[...], wup_ref[:, cols])
        val = _dot(hb_sc[FF_HALO:FF_HALO + tm, :], wup_ref[:, vcols])
        gate = cb_ref[:, cols] + cw_ref[1:2, cols] * g_ext[FF_HALO:FF_HALO + tm]
        gate = gate + cw_ref[0:1, cols] * g_ext[FF_HALO - 1:FF_HALO - 1 + tm]
        gate = gate + cw_ref[2:3, cols] * g_ext[FF_HALO + 1:FF_HALO + 1 + tm]
        part = _dot((_silu(gate) * val).astype(BF16), wdn_ref[cols, :])
        if j == 0:
            acc_sc[...] = part
        else:
            acc_sc[...] += part
    o_ref[...] = _layer_norm(DEEPNORM_ALPHA * h + acc_sc[...], g_ref[...], b_ref[...])


def _ffn(h, seq, wup, cw, cb, wdn, g, b):
    t = h.shape[0]
    tm = FFN_TOKEN_TILE
    assert seq % tm == 0 and tm % FF_HALO == 0 and D_FF % FF_TILE == 0
    tiles_per_seq = seq // tm
    hb = tm // FF_HALO
    nhb = t // FF_HALO
    row = pl.BlockSpec((tm, D_MODEL), lambda i: (i, 0))
    prev = pl.BlockSpec((FF_HALO, D_MODEL), lambda i: (jnp.maximum(i * hb - 1, 0), 0))
    nxt = pl.BlockSpec((FF_HALO, D_MODEL), lambda i: (jnp.minimum((i + 1) * hb, nhb - 1), 0))
    full = lambda a: pl.BlockSpec(a.shape, lambda i: (0,) * a.ndim, pipeline_mode=pl.Buffered(1))
    return pl.pallas_call(
        functools.partial(_ffn_kernel, tiles_per_seq),
        grid=(t // tm,),
        in_specs=[prev, row, nxt, full(wup), full(cw), full(cb), full(wdn), full(g), full(b)],
        out_specs=row,
        out_shape=jax.ShapeDtypeStruct((t, D_MODEL), F32),
        scratch_shapes=[pltpu.VMEM((tm + 2 * FF_HALO, D_MODEL), BF16), pltpu.VMEM((tm, D_MODEL), F32)],
        compiler_params=pltpu.CompilerParams(dimension_semantics=("arbitrary",), vmem_limit_bytes=VMEM_LIMIT),
        name="ffn_ln",
    )(h, h, h, wup, cw, cb, wdn, g, b)


def _rope_tables(seq):
    half = HEAD_DIM // 2
    inv_freq = ROPE_THETA ** (-jnp.arange(0, HEAD_DIM, 2, dtype=F32) / HEAD_DIM)
    ang = jnp.arange(seq, dtype=F32)[:, None] * inv_freq[None, :]
    cos, sin = jnp.cos(ang), jnp.sin(ang)
    reps = LANES // HEAD_DIM
    return jnp.tile(jnp.concatenate([cos, cos], axis=1), (1, reps)), jnp.tile(jnp.concatenate([-sin, sin], axis=1), (1, reps))


def _per_group(fwd, bwd, offset_f, offset_b):
    out = jnp.zeros((SSD_GROUPS, LANES), F32)
    for g in range(SSD_GROUPS):
        sl = slice(g * HEADS_PER_GROUP, (g + 1) * HEADS_PER_GROUP)
        if fwd is not None:
            out = out.at[g, offset_f:offset_f + HEADS_PER_GROUP].set(fwd[sl])
        if bwd is not None:
            out = out.at[g, offset_b:offset_b + HEADS_PER_GROUP].set(bwd[sl])
    return out


def _layer_params(w_in, ssd_a_log, ssd_dt_bias, ssd_d, w_out, w_up, w_down):
    o_k = ATTN_WIDTH + KV_WIDTH
    o_z = o_k + KV_WIDTH
    o_x = o_z + SSD_WIDTH
    o_dt = o_x + XBC_WIDTH
    hp = HEADS_PER_GROUP
    wdt_src = w_in[:, o_dt:]
    wdt = jnp.zeros((D_MODEL, SSD_GROUPS * LANES), F32)
    for g in range(SSD_GROUPS):
        wdt = wdt.at[:, g * LANES:g * LANES + hp].set(wdt_src[:, g * hp:(g + 1) * hp])
        wdt = wdt.at[:, g * LANES + hp:g * LANES + 2 * hp].set(wdt_src[:, SSD_HEADS + g * hp:SSD_HEADS + (g + 1) * hp])
        wdt = wdt.at[:, g * LANES + 2 * hp:g * LANES + 4 * hp].set(wdt[:, g * LANES:g * LANES + 2 * hp])
    gp = jnp.zeros((SSD_GROUPS, 8, LANES), F32)
    dt_bias = _per_group(ssd_dt_bias[0], ssd_dt_bias[1], 0, hp)
    gp = gp.at[:, 0, :].set(dt_bias.at[:, 2 * hp:4 * hp].set(dt_bias[:, 0:2 * hp]))
    gp = gp.at[:, 1, :].set(_per_group(ssd_a_log[0], ssd_a_log[1], 0, hp))
    return dict(
        wqk=w_in[:, :o_k].astype(BF16), wv=w_in[:, o_k:o_z].astype(BF16), wz=w_in[:, o_z:o_x].astype(BF16),
        wx=w_in[:, o_x:o_dt].astype(BF16), wdt=wdt.astype(BF16), gp=gp,
        d_exp=jnp.repeat(ssd_d, SSD_HEAD_DIM)[None, :],
        wa=w_out[:ATTN_WIDTH].astype(BF16), ws=w_out[ATTN_WIDTH:].astype(BF16),
        wup=w_up.astype(BF16), wdn=w_down.astype(BF16))


def _trunk(x, ln_in_g, ln_in_b, layers):
    b, seq, _ = x.shape
    cos_t, sin_t = _rope_tables(seq)
    h = x.reshape(b * seq, D_MODEL)
    lng, lnb = ln_in_g[None, :], ln_in_b[None, :]
    for li, p in enumerate(layers):
        outs = _inproj(h, seq, li == 0, lng, lnb, p["wqk"], p["wv"], p["wz"], p["wx"], p["wdt"], cos_t, sin_t)
        if li == 0:
            h, outs = outs[0], outs[1:]
        q, kv, z, xbc, dt = outs
        three = lambda a: a.reshape(b, seq, a.shape[-1])
        attn = _attention(three(q), three(kv), p["sink"])
        y, ss = _ssd(three(xbc), three(z), three(dt), p["conv_w"], p["conv_b"], p["gp"], p["d_exp"], p["norm_g"])
        h = _outproj(attn.reshape(b * seq, -1), y.reshape(b * seq, -1), ss.reshape(b * seq, -1), h,
                     p["wa"], p["ws"], p["ln1_g"], p["ln1_b"])
        h = _ffn(h, seq, p["wup"], p["ffn_cw"], p["ffn_cb"], p["wdn"], p["ln2_g"], p["ln2_b"])
    return h.reshape(b, seq, D_MODEL)


def kernel(x_prompt, x_sample, ln_in_g, ln_in_b, w_in, attn_sink, ssd_conv_w, ssd_conv_b, ssd_a_log, ssd_dt_bias, ssd_d, ssd_norm_g, w_out, ln1_g, ln1_b, w_up, ffn_conv_w, ffn_conv_b, w_down, ln2_g, ln2_b):
    layers = []
    for i in range(w_in.shape[0]):
        p = _layer_params(w_in[i], ssd_a_log[i], ssd_dt_bias[i], ssd_d[i], w_out[i], w_up[i], w_down[i])
        p.update(sink=attn_sink[i], conv_w=ssd_conv_w[i], conv_b=ssd_conv_b[i][None, :], norm_g=ssd_norm_g[i][None, :],
                 ln1_g=ln1_g[i][None, :], ln1_b=ln1_b[i][None, :], ffn_cw=ffn_conv_w[i], ffn_cb=ffn_conv_b[i][None, :],
                 ln2_g=ln2_g[i][None, :], ln2_b=ln2_b[i][None, :])
        layers.append(p)
    return (_trunk(x_prompt, ln_in_g, ln_in_b, layers), _trunk(x_sample, ln_in_g, ln_in_b, layers))
```

```python
import functools
import math

import jax
import jax.numpy as jnp
import numpy as np
from jax import lax
from jax.experimental import pallas as pl
from jax.experimental.pallas import tpu as pltpu

F32 = jnp.float32
BF16 = jnp.bfloat16

D_MODEL = 1024
DEPTH = 4
HEAD_DIM = 64
N_Q_HEADS = 8
N_KV_HEADS = 2
ATTN_WIDTH = N_Q_HEADS * HEAD_DIM
KV_WIDTH = N_KV_HEADS * HEAD_DIM
WINDOW = 128
ROPE_THETA = 10000.0
SSD_HEADS = 16
SSD_HEAD_DIM = 64
SSD_WIDTH = SSD_HEADS * SSD_HEAD_DIM
SSD_GROUPS = 2
HEADS_PER_GROUP = SSD_HEADS // SSD_GROUPS
GROUP_WIDTH = HEADS_PER_GROUP * SSD_HEAD_DIM
SSD_STATE = 128
SSD_CONV = 5
CHUNK = 128
XBC_WIDTH = SSD_WIDTH + 2 * SSD_GROUPS * SSD_STATE
D_FF = 2816
FFN_CONV = 3
DEEPNORM_ALPHA = (2 * DEPTH) ** 0.25
EPS = 1e-5
MASK_VALUE = -1e30

LANES = 128
VMEM_LIMIT = 56 * 1024 * 1024
TOKEN_TILE = 512
FFN_TOKEN_TILE = 1024
ATTN_BQ = 256
FF_TILE = 256
FF_HALO = 16
CONV_PAD = 16


def _dot(a, b):
    return jnp.dot(a, b, preferred_element_type=F32)


def _dot_nt(a, b):
    return lax.dot_general(a, b, (((1,), (1,)), ((), ())), preferred_element_type=F32)


def _layer_norm(x, g, b):
    mu = jnp.mean(x, axis=-1, keepdims=True)
    xc = x - mu
    var = jnp.mean(xc * xc, axis=-1, keepdims=True)
    return xc * lax.rsqrt(var + EPS) * g + b


def _silu(x):
    return x * (1.0 / (1.0 + jnp.exp(-x)))


def _split3(x):
    hi = x.astype(BF16)
    r1 = x - hi.astype(F32)
    mid = r1.astype(BF16)
    lo = (r1 - mid.astype(F32)).astype(BF16)
    return jnp.concatenate([hi, mid, lo], axis=1)


def _inproj_kernel(apply_ln, x_ref, lng_ref, lnb_ref, wqk_ref, wv_ref, wz_ref, wx_ref, wdt_ref,
                   cos_ref, sin_ref, *outs):
    if apply_ln:
        h_ref, q_ref, kv_ref, z_ref, xbc_ref, dt_ref = outs
    else:
        q_ref, kv_ref, z_ref, xbc_ref, dt_ref = outs
    x = x_ref[...]
    if apply_ln:
        x = _layer_norm(x, lng_ref[...], lnb_ref[...])
        h_ref[...] = x
    xb = x.astype(BF16)
    tm = x.shape[0]
    cos = cos_ref[...]
    sin = sin_ref[...]
    lane = lax.broadcasted_iota(jnp.int32, (tm, LANES), 1)
    first_half = (lane % HEAD_DIM) < (HEAD_DIM // 2)
    low_head = lane < HEAD_DIM

    def rope(t):
        swapped = jnp.where(first_half, pltpu.roll(t, LANES - HEAD_DIM // 2, 1), pltpu.roll(t, HEAD_DIM // 2, 1))
        return t * cos + swapped * sin

    def both_halves(t):
        sw = pltpu.roll(t, HEAD_DIM, 1)
        return jnp.where(low_head, t, sw), jnp.where(low_head, sw, t)

    qk = _dot(xb, wqk_ref[...])
    scale = HEAD_DIM ** -0.5
    for j in range(ATTN_WIDTH // LANES):
        q_ref[:, j * LANES:(j + 1) * LANES] = (rope(qk[:, j * LANES:(j + 1) * LANES]) * scale).astype(BF16)
    k0, k1 = both_halves(rope(qk[:, ATTN_WIDTH:ATTN_WIDTH + KV_WIDTH]))
    v0, v1 = both_halves(_dot(xb, wv_ref[...]))
    kv_ref[:, 0 * LANES:1 * LANES] = k0.astype(BF16)
    kv_ref[:, 1 * LANES:2 * LANES] = k1.astype(BF16)
    kv_ref[:, 2 * LANES:3 * LANES] = v0.astype(BF16)
    kv_ref[:, 3 * LANES:4 * LANES] = v1.astype(BF16)
    z_ref[...] = _dot(xb, wz_ref[...]).astype(BF16)
    xbc_ref[...] = _dot(xb, wx_ref[...]).astype(BF16)
    dt_ref[...] = _dot(xb, wdt_ref[...])


def _inproj(x2d, seq, apply_ln, lng, lnb, wqk, wv, wz, wx, wdt, cos_t, sin_t):
    t = x2d.shape[0]
    tm = TOKEN_TILE
    assert t % tm == 0 and seq % tm == 0
    tiles_per_seq = seq // tm
    row = lambda w: pl.BlockSpec((tm, w), lambda i: (i, 0))
    full = lambda a: pl.BlockSpec(a.shape, lambda i: (0,) * a.ndim)
    tab = pl.BlockSpec((tm, LANES), lambda i: (i % tiles_per_seq, 0))
    out_shape = [jax.ShapeDtypeStruct((t, ATTN_WIDTH), BF16), jax.ShapeDtypeStruct((t, 4 * LANES), BF16),
                 jax.ShapeDtypeStruct((t, SSD_WIDTH), BF16), jax.ShapeDtypeStruct((t, XBC_WIDTH), BF16),
                 jax.ShapeDtypeStruct((t, SSD_GROUPS * LANES), F32)]
    out_specs = [row(ATTN_WIDTH), row(4 * LANES), row(SSD_WIDTH), row(XBC_WIDTH), row(SSD_GROUPS * LANES)]
    if apply_ln:
        out_shape = [jax.ShapeDtypeStruct((t, D_MODEL), F32)] + out_shape
        out_specs = [row(D_MODEL)] + out_specs
    return pl.pallas_call(
        functools.partial(_inproj_kernel, apply_ln),
        grid=(t // tm,),
        in_specs=[row(D_MODEL), full(lng), full(lnb), full(wqk), full(wv), full(wz), full(wx), full(wdt), tab, tab],
        out_specs=out_specs,
        out_shape=out_shape,
        compiler_params=pltpu.CompilerParams(dimension_semantics=("arbitrary",), vmem_limit_bytes=VMEM_LIMIT),
        name="in_proj_ln" if apply_ln else "in_proj",
    )(x2d, lng, lnb, wqk, wv, wz, wx, wdt, cos_t, sin_t)


def _attn_kernel(seq, sink_ref, q_ref, kl_ref, km_ref, kr_ref, vl_ref, vm_ref, vr_ref, o_ref):
    bq = q_ref.shape[1]
    sub = WINDOW
    ks = sub + 2 * WINDOW
    start = pl.program_id(1) * bq
    qi = lax.broadcasted_iota(jnp.int32, (sub, ks), 0)
    kc = lax.broadcasted_iota(jnp.int32, (sub, ks), 1) - WINDOW
    band = jnp.abs(kc - qi) <= WINDOW
    lane = lax.broadcasted_iota(jnp.int32, (sub, LANES), 1)
    low_head = lane < HEAD_DIM
    group = N_Q_HEADS // N_KV_HEADS
    for h in range(N_KV_HEADS):
        ksl = slice(h * LANES, (h + 1) * LANES)
        kk = jnp.concatenate([kl_ref[0, :, ksl], km_ref[0, :, ksl], kr_ref[0, :, ksl]], axis=0)
        vv = jnp.concatenate([vl_ref[0, :, ksl], vm_ref[0, :, ksl], vr_ref[0, :, ksl]], axis=0)
        for sb in range(bq // sub):
            rows = slice(sb * sub, (sb + 1) * sub)
            kpos = kc + (start + sb * sub)
            allow = band & (kpos >= 0) & (kpos < seq)
            qs = []
            for r in range(group):
                head = group * h + r
                qt = q_ref[0, rows, (head // 2) * LANES:(head // 2 + 1) * LANES]
                keep = low_head if head % 2 == 0 else jnp.logical_not(low_head)
                qs.append(jnp.where(keep, qt, jnp.zeros_like(qt)))
            sc = _dot_nt(jnp.concatenate(qs, axis=0), kk[sb * sub:sb * sub + ks])
            ps, inv = [], []
            for r in range(group):
                sink = sink_ref[group * h + r]
                s = jnp.where(allow, sc[r * sub:(r + 1) * sub], MASK_VALUE)
                m = jnp.maximum(jnp.max(s, axis=-1, keepdims=True), sink)
                p = jnp.exp(s - m)
                denom = jnp.sum(p, axis=-1, keepdims=True) + jnp.exp(sink - m)
                ps.append(p.astype(BF16))
                inv.append(1.0 / denom)
            pv = _dot(jnp.concatenate(ps, axis=0), vv[sb * sub:sb * sub + ks])
            for pair in range(group // 2):
                even = pv[(2 * pair) * sub:(2 * pair + 1) * sub] * inv[2 * pair]
                odd = pv[(2 * pair + 1) * sub:(2 * pair + 2) * sub] * inv[2 * pair + 1]
                col = (group * h) // 2 + pair
                o_ref[0, rows, col * LANES:(col + 1) * LANES] = jnp.where(low_head, even, odd).astype(BF16)


def _attention(q, kv, sink):
    b, seq, _ = q.shape
    bq = ATTN_BQ
    assert seq % bq == 0 and bq % WINDOW == 0
    r = bq // WINDOW
    nwb = seq // WINDOW
    side = lambda lane_blk, f: pl.BlockSpec((1, WINDOW, 2 * LANES), lambda bi, i: (bi, f(i), lane_blk))
    mid = lambda lane_blk: pl.BlockSpec((1, bq, 2 * LANES), lambda bi, i: (bi, i, lane_blk))
    left = lambda i: jnp.maximum(i * r - 1, 0)
    right = lambda i: jnp.minimum((i + 1) * r, nwb - 1)
    return pl.pallas_call(
        functools.partial(_attn_kernel, seq),
        grid=(b, seq // bq),
        in_specs=[pl.BlockSpec(memory_space=pltpu.SMEM),
                  pl.BlockSpec((1, bq, ATTN_WIDTH), lambda bi, i: (bi, i, 0)),
                  side(0, left), mid(0), side(0, right), side(1, left), mid(1), side(1, right)],
        out_specs=pl.BlockSpec((1, bq, ATTN_WIDTH), lambda bi, i: (bi, i, 0)),
        out_shape=jax.ShapeDtypeStruct((b, seq, ATTN_WIDTH), BF16),
        compiler_params=pltpu.CompilerParams(dimension_semantics=("arbitrary", "arbitrary"),
                                             vmem_limit_bytes=VMEM_LIMIT),
        name="window_attn",
    )(sink, q, kv, kv, kv, kv, kv, kv)


def _ssd_kernel(x_ref, bm_ref, cm_ref, z_ref, dt_ref, cwx_ref, cwb_ref, cwc_ref, cbx_ref, cbb_ref, cbc_ref,
                gp_ref, dexp_ref, ng_ref, tril_ref, triu_ref, ef_ref, eb_ref, sh_ref, y_ref, ss_ref,
                pad_sc, xs_sc, bs_sc, cs_sc, bt_sc, yacc_sc, locb_sc, ef_sc, er_sc, keepf_sc, keepb_sc, fr_sc, rows_sc,
                sf_sc, sb_sc):
    seq = x_ref.shape[1]
    nchunks = seq // CHUNK
    gw = GROUP_WIDTH
    hp = HEADS_PER_GROUP
    pad = CONV_PAD
    taps = [k for k in range(SSD_CONV) if k != SSD_CONV // 2]

    zeros = jnp.zeros((pad, gw + 2 * LANES), BF16)
    pad_sc[0:pad, :] = zeros
    pad_sc[pad + seq:pad + seq + pad, :] = zeros

    def fill(c, carry):
        r0 = pl.multiple_of(c * CHUNK, CHUNK)
        pad_sc[pl.ds(pad + r0, CHUNK), 0:gw] = x_ref[0, pl.ds(r0, CHUNK), :]
        pad_sc[pl.ds(pad + r0, CHUNK), gw:gw + LANES] = bm_ref[0, pl.ds(r0, CHUNK), :]
        pad_sc[pl.ds(pad + r0, CHUNK), gw + LANES:gw + 2 * LANES] = cm_ref[0, pl.ds(r0, CHUNK), :]
        return carry

    lax.fori_loop(0, nchunks, fill, 0)

    lane_row = lax.broadcasted_iota(jnp.int32, (1, LANES), 1)
    dt_bias = gp_ref[0, 0:1, :]
    a_row = jnp.where(lane_row < 2 * hp, -jnp.exp(gp_ref[0, 1:2, :]), 0.0)
    dexp = dexp_ref[...]
    tril = tril_ref[...]
    triu = triu_ref[...]
    qi = lax.broadcasted_iota(jnp.int32, (CHUNK, CHUNK), 0)
    si = lax.broadcasted_iota(jnp.int32, (CHUNK, CHUNK), 1)
    lower = si <= qi
    before = si < qi
    after = si > qi
    low_head = si < SSD_HEAD_DIM
    fwd_lane = si < hp

    def blockdiag(t):
        return jnp.concatenate([jnp.where(low_head, t, 0.0), jnp.where(low_head, 0.0, t)], axis=0).astype(BF16)

    def head_rows(v, e_ref):
        return _dot(_split3(jnp.broadcast_to(v, (8, LANES))), e_ref[...])

    def chunk_scalars(r0):
        x = dt_ref[0, pl.ds(r0, CHUNK), :] + dt_bias
        dt = jnp.maximum(x, 0.0) + jnp.log1p(jnp.exp(-jnp.abs(x)))
        pieces = _split3(dt * a_row)
        fp = _dot(tril, pieces)
        rp = _dot(triu, pieces)
        f = fp[:, 0:LANES] + fp[:, LANES:2 * LANES] + fp[:, 2 * LANES:3 * LANES]
        r = rp[:, 0:LANES] + rp[:, LANES:2 * LANES] + rp[:, 2 * LANES:3 * LANES]
        return dt, f, r

    def expand(v, e_ref):
        return _dot(_split3(v), e_ref[...])

    def conv(c, carry):
        r0 = pl.multiple_of(c * CHUNK, CHUNK)
        shifted = _dot(sh_ref[...], pad_sc[pl.ds(r0, CHUNK + 2 * pad), :])
        centre = pad_sc[pl.ds(r0 + pad, CHUNK), :].astype(F32)

        def one(lo, hi, w_ref, b_ref):
            acc = b_ref[...] + w_ref[SSD_CONV // 2:SSD_CONV // 2 + 1, :] * centre[:, lo:hi]
            for j, k in enumerate(taps):
                acc = acc + w_ref[k:k + 1, :] * shifted[j * CHUNK:(j + 1) * CHUNK, lo:hi]
            return _silu(acc)

        xs_sc[pl.ds(r0, CHUNK), :] = one(0, gw, cwx_ref, cbx_ref)
        bconv = one(gw, gw + LANES, cwb_ref, cbb_ref)
        bs_sc[pl.ds(r0, CHUNK), :] = bconv.astype(BF16)
        bt_sc[c] = bconv.T
        cs_sc[pl.ds(r0, CHUNK), :] = one(gw + LANES, gw + 2 * LANES, cwc_ref, cbc_ref).astype(BF16)

        dt, f, r = chunk_scalars(r0)
        fr = jnp.where(fwd_lane, f, r)
        fr_sc[c] = fr
        pt = jnp.where(si < 2 * hp, fr, dt).T
        cums_t = pt[0:2 * hp, :]
        dt_t = pt[2 * hp:4 * hp, :]
        dte_t = jnp.exp(jnp.where(qi[0:2 * hp] < hp, cums_t[:, CHUNK - 1:CHUNK] - cums_t, cums_t[:, 0:1] - cums_t)) * dt_t
        rows_sc[c, 0:4 * hp, :] = pt[0:4 * hp, :]
        rows_sc[c, 4 * hp:6 * hp, :] = dte_t
        keepf_sc[c] = head_rows(jnp.exp(f[CHUNK - 1:CHUNK, :]), ef_ref)
        keepb_sc[c] = head_rows(jnp.exp(r[0:1, :]), eb_ref)
        ef_sc[c] = expand(jnp.exp(f), ef_ref)
        er_sc[c] = expand(jnp.exp(r), eb_ref)
        return carry

    lax.fori_loop(0, nchunks, conv, 0, unroll=4)

    sf_sc[...] = jnp.zeros_like(sf_sc)
    sb_sc[...] = jnp.zeros_like(sb_sc)

    def ascend(c, carry):
        r0 = pl.multiple_of(c * CHUNK, CHUNK)
        fr = fr_sc[c]
        cums_t = rows_sc[c, 0:2 * hp, :]
        dt_t = rows_sc[c, 2 * hp:4 * hp, :]
        dte_t = rows_sc[c, 4 * hp:6 * hp, :]
        keep_f = keepf_sc[c]
        xs = xs_sc[pl.ds(r0, CHUNK), :]
        cc = cs_sc[pl.ds(r0, CHUNK), :]
        cb = _dot_nt(cc, bs_sc[pl.ds(r0, CHUNK), :])
        yoff = _dot(cc, sf_sc[...].astype(BF16)) * ef_sc[c]
        bt = bt_sc[c]
        for pair in range(hp // 2):
            ms, bfs, bbs = [], [], []
            for i in (2 * pair, 2 * pair + 1):
                seg = jnp.where(lower, fr[:, i:i + 1] - cums_t[i:i + 1, :], fr[:, hp + i:hp + i + 1] - cums_t[hp + i:hp + i + 1, :])
                dtf_row = dt_t[i:i + 1, :]
                dtb_row = dt_t[hp + i:hp + i + 1, :]
                w = jnp.where(before, dtf_row, jnp.where(after, dtb_row, dtf_row + dtb_row))
                ms.append((cb * jnp.exp(seg) * w).astype(BF16))
                bfs.append((bt * dte_t[i:i + 1, :]).astype(BF16))
                bbs.append((bt * dte_t[hp + i:hp + i + 1, :]).astype(BF16))
            sl = slice(pair * LANES, (pair + 1) * LANES)
            lhs = jnp.concatenate([jnp.concatenate(ms, axis=1), jnp.concatenate(bfs, axis=1),
                                   jnp.concatenate(bbs, axis=1)], axis=0)
            res = _dot(lhs, blockdiag(xs[:, sl]))
            yacc_sc[pl.ds(r0, CHUNK), sl] = res[0:CHUNK] + yoff[:, sl] + xs[:, sl] * dexp[:, sl]
            sf_sc[:, sl] = sf_sc[:, sl] * keep_f[0:1, sl] + res[CHUNK:2 * CHUNK]
            locb_sc[c, :, sl] = res[2 * CHUNK:3 * CHUNK]
        return carry

    lax.fori_loop(0, nchunks, ascend, 0, unroll=2)

    def descend(j, carry):
        c = nchunks - 1 - j
        r0 = pl.multiple_of(c * CHUNK, CHUNK)
        sb = sb_sc[...]
        y = yacc_sc[pl.ds(r0, CHUNK), :] + _dot(cs_sc[pl.ds(r0, CHUNK), :], sb.astype(BF16)) * er_sc[c]
        sb_sc[...] = sb * keepb_sc[c][0:1, :] + locb_sc[c]
        gated = y * _silu(z_ref[0, pl.ds(r0, CHUNK), :].astype(F32))
        ss = jnp.sum(gated * gated, axis=1, keepdims=True)
        ss_ref[0, pl.ds(r0, CHUNK), :] = jnp.broadcast_to(ss, (CHUNK, LANES))
        y_ref[0, pl.ds(r0, CHUNK), :] = (gated * ng_ref[...]).astype(BF16)
        return carry

    lax.fori_loop(0, nchunks, descend, 0, unroll=4)


def _ssd_constants():
    idx = np.arange(CHUNK)
    tril = (idx[None, :] <= idx[:, None]).astype(np.float32)
    triu = (idx[None, :] >= idx[:, None]).astype(np.float32)
    rows = np.arange(3 * LANES)[:, None] % LANES
    cols = np.arange(GROUP_WIDTH)[None, :] // SSD_HEAD_DIM
    ef = (rows == cols).astype(np.float32)
    eb = (rows == cols + HEADS_PER_GROUP).astype(np.float32)
    taps = [k for k in range(SSD_CONV) if k != SSD_CONV // 2]
    src = np.arange(CHUNK + 2 * CONV_PAD)[None, :]
    shift = np.concatenate([(src == idx[:, None] + CONV_PAD + k - SSD_CONV // 2) for k in taps], axis=0).astype(np.float32)
    return tuple(jnp.asarray(a, dtype=BF16) for a in (tril, triu, ef, eb, shift))


def _ssd(xbc, z, dt, conv_w, conv_b, gp, d_exp, norm_g):
    b, seq, _ = xbc.shape
    assert seq % CHUNK == 0
    gw = GROUP_WIDTH
    nx = SSD_WIDTH // LANES
    nchunks = seq // CHUNK
    consts = _ssd_constants()
    seq_blk = lambda w, f: pl.BlockSpec((1, seq, w), lambda bi, g: (bi, 0, f(g)))
    par = lambda rows, w, f: pl.BlockSpec((rows, w), lambda bi, g: (0, f(g)))
    full = lambda a: pl.BlockSpec(a.shape, lambda bi, g: (0,) * a.ndim)
    return pl.pallas_call(
        _ssd_kernel,
        grid=(b, SSD_GROUPS),
        in_specs=[seq_blk(gw, lambda g: g), seq_blk(LANES, lambda g: nx + g), seq_blk(LANES, lambda g: nx + SSD_GROUPS + g),
                  seq_blk(gw, lambda g: g), seq_blk(LANES, lambda g: g),
                  par(SSD_CONV, gw, lambda g: g), par(SSD_CONV, LANES, lambda g: nx + g),
                  par(SSD_CONV, LANES, lambda g: nx + SSD_GROUPS + g),
                  par(1, gw, lambda g: g), par(1, LANES, lambda g: nx + g), par(1, LANES, lambda g: nx + SSD_GROUPS + g),
                  pl.BlockSpec((1, 8, LANES), lambda bi, g: (g, 0, 0)), par(1, gw, lambda g: g), par(1, gw, lambda g: g)]
                 + [full(a) for a in consts],
        out_specs=[seq_blk(gw, lambda g: g), seq_blk(LANES, lambda g: g)],
        out_shape=[jax.ShapeDtypeStruct((b, seq, SSD_WIDTH), BF16), jax.ShapeDtypeStruct((b, seq, SSD_GROUPS * LANES), F32)],
        scratch_shapes=[pltpu.VMEM((seq + 2 * CONV_PAD, gw + 2 * LANES), BF16), pltpu.VMEM((seq, gw), F32),
                        pltpu.VMEM((seq, LANES), BF16), pltpu.VMEM((seq, LANES), BF16),
                        pltpu.VMEM((nchunks, SSD_STATE, CHUNK), F32), pltpu.VMEM((seq, gw), F32),
                        pltpu.VMEM((nchunks, SSD_STATE, gw), F32), pltpu.VMEM((nchunks, CHUNK, gw), F32),
                        pltpu.VMEM((nchunks, CHUNK, gw), F32), pltpu.VMEM((nchunks, 8, gw), F32), pltpu.VMEM((nchunks, 8, gw), F32),
                        pltpu.VMEM((nchunks, CHUNK, LANES), F32), pltpu.VMEM((nchunks, 6 * HEADS_PER_GROUP, LANES), F32),
                        pltpu.VMEM((SSD_STATE, gw), F32), pltpu.VMEM((SSD_STATE, gw), F32)],
        compiler_params=pltpu.CompilerParams(dimension_semantics=("arbitrary", "arbitrary"),
                                             vmem_limit_bytes=VMEM_LIMIT),
        name="bidir_ssd",
    )(xbc, xbc, xbc, z, dt, conv_w, conv_w, conv_w, conv_b, conv_b, conv_b, gp, d_exp, norm_g, *consts)


def _outproj_kernel(attn_ref, y_ref, ss_ref, h_ref, wa_ref, ws_ref, g_ref, b_ref, o_ref):
    ss = ss_ref[...]
    total = ss[:, 0:1]
    for g in range(1, SSD_GROUPS):
        total = total + ss[:, g * LANES:g * LANES + 1]
    rs = lax.rsqrt(total * (1.0 / SSD_WIDTH) + EPS)
    mix = _dot(attn_ref[...], wa_ref[...]) + rs * _dot(y_ref[...], ws_ref[...])
    o_ref[...] = _layer_norm(DEEPNORM_ALPHA * h_ref[...] + mix, g_ref[...], b_ref[...])


def _outproj(attn, y, ss, h, wa, ws, g, b):
    t = h.shape[0]
    tm = TOKEN_TILE
    row = lambda w: pl.BlockSpec((tm, w), lambda i: (i, 0))
    full = lambda a: pl.BlockSpec(a.shape, lambda i: (0,) * a.ndim)
    return pl.pallas_call(
        _outproj_kernel,
        grid=(t // tm,),
        in_specs=[row(ATTN_WIDTH), row(SSD_WIDTH), row(SSD_GROUPS * LANES), row(D_MODEL), full(wa), full(ws), full(g), full(b)],
        out_specs=row(D_MODEL),
        out_shape=jax.ShapeDtypeStruct((t, D_MODEL), F32),
        compiler_params=pltpu.CompilerParams(dimension_semantics=("arbitrary",), vmem_limit_bytes=VMEM_LIMIT),
        name="out_proj_ln",
    )(attn, y, ss, h, wa, ws, g, b)


def _ffn_kernel(tiles_per_seq, hp_ref, h_ref, hn_ref, wup_ref, cw_ref, cb_ref, wdn_ref, g_ref, b_ref, o_ref,
                hb_sc, act_sc):
    tm = h_ref.shape[0]
    pos = pl.program_id(0) % tiles_per_seq
    h = h_ref[...]
    hb_sc[0:FF_HALO, :] = jnp.where(pos == 0, 0.0, hp_ref[...]).astype(BF16)
    hb_sc[FF_HALO:FF_HALO + tm, :] = h.astype(BF16)
    hb_sc[FF_HALO + tm:, :] = jnp.where(pos == tiles_per_seq - 1, 0.0, hn_ref[...]).astype(BF16)
    for j in range(D_FF // FF_TILE):
        cols = slice(j * FF_TILE, (j + 1) * FF_TILE)
        vcols = slice(D_FF + j * FF_TILE, D_FF + (j + 1) * FF_TILE)
        g_ext = _dot(hb_sc[...], wup_ref[:, cols])
        val = _dot(hb_sc[FF_HALO:FF_HALO + tm, :], wup_ref[:, vcols])
        gate = cb_ref[:, cols] + cw_ref[1:2, cols] * g_ext[FF_HALO:FF_HALO + tm]
        gate = gate + cw_ref[0:1, cols] * g_ext[FF_HALO - 1:FF_HALO - 1 + tm]
        gate = gate + cw_ref[2:3, cols] * g_ext[FF_HALO + 1:FF_HALO + 1 + tm]
        act_sc[:, cols] = (_silu(gate) * val).astype(BF16)
    ffn = _dot(act_sc[...], wdn_ref[...])
    o_ref[...] = _layer_norm(DEEPNORM_ALPHA * h + ffn, g_ref[...], b_ref[...])


def _ffn(h, seq, wup, cw, cb, wdn, g, b):
    t = h.shape[0]
    tm = FFN_TOKEN_TILE
    assert seq % tm == 0 and tm % FF_HALO == 0 and D_FF % FF_TILE == 0
    tiles_per_seq = seq // tm
    hb = tm // FF_HALO
    nhb = t // FF_HALO
    row = pl.BlockSpec((tm, D_MODEL), lambda i: (i, 0))
    prev = pl.BlockSpec((FF_HALO, D_MODEL), lambda i: (jnp.maximum(i * hb - 1, 0), 0))
    nxt = pl.BlockSpec((FF_HALO, D_MODEL), lambda i: (jnp.minimum((i + 1) * hb, nhb - 1), 0))
    full = lambda a: pl.BlockSpec(a.shape, lambda i: (0,) * a.ndim, pipeline_mode=pl.Buffered(1))
    return pl.pallas_call(
        functools.partial(_ffn_kernel, tiles_per_seq),
        grid=(t // tm,),
        in_specs=[prev, row, nxt, full(wup), full(cw), full(cb), full(wdn), full(g), full(b)],
        out_specs=row,
        out_shape=jax.ShapeDtypeStruct((t, D_MODEL), F32),
        scratch_shapes=[pltpu.VMEM((tm + 2 * FF_HALO, D_MODEL), BF16), pltpu.VMEM((tm, D_FF), BF16)],
        compiler_params=pltpu.CompilerParams(dimension_semantics=("arbitrary",), vmem_limit_bytes=VMEM_LIMIT),
        name="ffn_ln",
    )(h, h, h, wup, cw, cb, wdn, g, b)


def _rope_tables(seq):
    half = HEAD_DIM // 2
    inv_freq = ROPE_THETA ** (-jnp.arange(0, HEAD_DIM, 2, dtype=F32) / HEAD_DIM)
    ang = jnp.arange(seq, dtype=F32)[:, None] * inv_freq[None, :]
    cos, sin = jnp.cos(ang), jnp.sin(ang)
    reps = LANES // HEAD_DIM
    return jnp.tile(jnp.concatenate([cos, cos], axis=1), (1, reps)), jnp.tile(jnp.concatenate([-sin, sin], axis=1), (1, reps))


def _per_group(fwd, bwd, offset_f, offset_b):
    out = jnp.zeros((SSD_GROUPS, LANES), F32)
    for g in range(SSD_GROUPS):
        sl = slice(g * HEADS_PER_GROUP, (g + 1) * HEADS_PER_GROUP)
        if fwd is not None:
            out = out.at[g, offset_f:offset_f + HEADS_PER_GROUP].set(fwd[sl])
        if bwd is not None:
            out = out.at[g, offset_b:offset_b + HEADS_PER_GROUP].set(bwd[sl])
    return out


def _layer_params(w_in, ssd_a_log, ssd_dt_bias, ssd_d, w_out, w_up, w_down):
    o_k = ATTN_WIDTH + KV_WIDTH
    o_z = o_k + KV_WIDTH
    o_x = o_z + SSD_WIDTH
    o_dt = o_x + XBC_WIDTH
    hp = HEADS_PER_GROUP
    wdt_src = w_in[:, o_dt:]
    wdt = jnp.zeros((D_MODEL, SSD_GROUPS * LANES), F32)
    for g in range(SSD_GROUPS):
        wdt = wdt.at[:, g * LANES:g * LANES + hp].set(wdt_src[:, g * hp:(g + 1) * hp])
        wdt = wdt.at[:, g * LANES + hp:g * LANES + 2 * hp].set(wdt_src[:, SSD_HEADS + g * hp:SSD_HEADS + (g + 1) * hp])
        wdt = wdt.at[:, g * LANES + 2 * hp:g * LANES + 4 * hp].set(wdt[:, g * LANES:g * LANES + 2 * hp])
    gp = jnp.zeros((SSD_GROUPS, 8, LANES), F32)
    dt_bias = _per_group(ssd_dt_bias[0], ssd_dt_bias[1], 0, hp)
    gp = gp.at[:, 0, :].set(dt_bias.at[:, 2 * hp:4 * hp].set(dt_bias[:, 0:2 * hp]))
    gp = gp.at[:, 1, :].set(_per_group(ssd_a_log[0], ssd_a_log[1], 0, hp))
    return dict(
        wqk=w_in[:, :o_k].astype(BF16), wv=w_in[:, o_k:o_z].astype(BF16), wz=w_in[:, o_z:o_x].astype(BF16),
        wx=w_in[:, o_x:o_dt].astype(BF16), wdt=wdt.astype(BF16), gp=gp,
        d_exp=jnp.repeat(ssd_d, SSD_HEAD_DIM)[None, :],
        wa=w_out[:ATTN_WIDTH].astype(BF16), ws=w_out[ATTN_WIDTH:].astype(BF16),
        wup=w_up.astype(BF16), wdn=w_down.astype(BF16))


def _trunk(x, ln_in_g, ln_in_b, layers):
    b, seq, _ = x.shape
    cos_t, sin_t = _rope_tables(seq)
    h = x.reshape(b * seq, D_MODEL)
    lng, lnb = ln_in_g[None, :], ln_in_b[None, :]
    for li, p in enumerate(layers):
        outs = _inproj(h, seq, li == 0, lng, lnb, p["wqk"], p["wv"], p["wz"], p["wx"], p["wdt"], cos_t, sin_t)
        if li == 0:
            h, outs = outs[0], outs[1:]
        q, kv, z, xbc, dt = outs
        three = lambda a: a.reshape(b, seq, a.shape[-1])
        attn = _attention(three(q), three(kv), p["sink"])
        y, ss = _ssd(three(xbc), three(z), three(dt), p["conv_w"], p["conv_b"], p["gp"], p["d_exp"], p["norm_g"])
        h = _outproj(attn.reshape(b * seq, -1), y.reshape(b * seq, -1), ss.reshape(b * seq, -1), h,
                     p["wa"], p["ws"], p["ln1_g"], p["ln1_b"])
        h = _ffn(h, seq, p["wup"], p["ffn_cw"], p["ffn_cb"], p["wdn"], p["ln2_g"], p["ln2_b"])
    return h.reshape(b, seq, D_MODEL)


def kernel(x_prompt, x_sample, ln_in_g, ln_in_b, w_in, attn_sink, ssd_conv_w, ssd_conv_b, ssd_a_log, ssd_dt_bias, ssd_d, ssd_norm_g, w_out, ln1_g, ln1_b, w_up, ffn_conv_w, ffn_conv_b, w_down, ln2_g, ln2_b):
    layers = []
    for i in range(w_in.shape[0]):
        p = _layer_params(w_in[i], ssd_a_log[i], ssd_dt_bias[i], ssd_d[i], w_out[i], w_up[i], w_down[i])
        p.update(sink=attn_sink[i], conv_w=ssd_conv_w[i], conv_b=ssd_conv_b[i][None, :], norm_g=ssd_norm_g[i][None, :],
                 ln1_g=ln1_g[i][None, :], ln1_b=ln1_b[i][None, :], ffn_cw=ffn_conv_w[i], ffn_cb=ffn_conv_b[i][None, :],
                 ln2_g=ln2_g[i][None, :], ln2_b=ln2_b[i][None, :])
        layers.append(p)
    return (_trunk(x_prompt, ln_in_g, ln_in_b, layers), _trunk(x_sample, ln_in_g, ln_in_b, layers))
```

```python
import functools
import math

import jax
import jax.numpy as jnp
import numpy as np
from jax import lax
from jax.experimental import pallas as pl
from jax.experimental.pallas import tpu as pltpu

F32 = jnp.float32
BF16 = jnp.bfloat16

D_MODEL = 1024
DEPTH = 4
HEAD_DIM = 64
N_Q_HEADS = 8
N_KV_HEADS = 2
ATTN_WIDTH = N_Q_HEADS * HEAD_DIM
KV_WIDTH = N_KV_HEADS * HEAD_DIM
WINDOW = 128
ROPE_THETA = 10000.0
SSD_HEADS = 16
SSD_HEAD_DIM = 64
SSD_WIDTH = SSD_HEADS * SSD_HEAD_DIM
SSD_GROUPS = 2
HEADS_PER_GROUP = SSD_HEADS // SSD_GROUPS
GROUP_WIDTH = HEADS_PER_GROUP * SSD_HEAD_DIM
SSD_STATE = 128
SSD_CONV = 5
CHUNK = 128
XBC_WIDTH = SSD_WIDTH + 2 * SSD_GROUPS * SSD_STATE
D_FF = 2816
FFN_CONV = 3
DEEPNORM_ALPHA = (2 * DEPTH) ** 0.25
EPS = 1e-5
MASK_VALUE = -1e30
LOG2E = math.log2(math.e)

LANES = 128
VMEM_LIMIT = 56 * 1024 * 1024
TOKEN_TILE = 1024
FFN_TOKEN_TILE = 1024
ATTN_BQ = 256
FF_TILE = 256
FFN_OUT_SLABS = 4
FF_HALO = 16
CONV_PAD = 16


def _dot(a, b):
    return jnp.dot(a, b, preferred_element_type=F32)


def _dot_nt(a, b):
    return lax.dot_general(a, b, (((1,), (1,)), ((), ())), preferred_element_type=F32)


def _layer_norm(x, g, b):
    mu = jnp.mean(x, axis=-1, keepdims=True)
    xc = x - mu
    var = jnp.mean(xc * xc, axis=-1, keepdims=True)
    return xc * lax.rsqrt(var + EPS) * g + b


def _silu(x):
    return x * (1.0 / (1.0 + jnp.exp(-x)))


def _split3(x):
    hi = x.astype(BF16)
    r1 = x - hi.astype(F32)
    mid = r1.astype(BF16)
    lo = (r1 - mid.astype(F32)).astype(BF16)
    return jnp.concatenate([hi, mid, lo], axis=1)


def _inproj_kernel(apply_ln, x_ref, lng_ref, lnb_ref, wqk_ref, wv_ref, wz_ref, wx_ref, wdt_ref,
                   cos_ref, sin_ref, *outs):
    if apply_ln:
        h_ref, q_ref, kv_ref, z_ref, xbc_ref, dt_ref = outs
    else:
        q_ref, kv_ref, z_ref, xbc_ref, dt_ref = outs
    x = x_ref[...]
    if apply_ln:
        x = _layer_norm(x, lng_ref[...], lnb_ref[...])
        h_ref[...] = x
    xb = x.astype(BF16)
    tm = x.shape[0]
    cos = cos_ref[...]
    sin = sin_ref[...]
    lane = lax.broadcasted_iota(jnp.int32, (tm, LANES), 1)
    first_half = (lane % HEAD_DIM) < (HEAD_DIM // 2)
    low_head = lane < HEAD_DIM

    def rope(t):
        swapped = jnp.where(first_half, pltpu.roll(t, LANES - HEAD_DIM // 2, 1), pltpu.roll(t, HEAD_DIM // 2, 1))
        return t * cos + swapped * sin

    def both_halves(t):
        sw = pltpu.roll(t, HEAD_DIM, 1)
        return jnp.where(low_head, t, sw), jnp.where(low_head, sw, t)

    qk = _dot(xb, wqk_ref[...])
    scale = HEAD_DIM ** -0.5
    for j in range(ATTN_WIDTH // LANES):
        q_ref[:, j * LANES:(j + 1) * LANES] = (rope(qk[:, j * LANES:(j + 1) * LANES]) * scale).astype(BF16)
    k0, k1 = both_halves(rope(qk[:, ATTN_WIDTH:ATTN_WIDTH + KV_WIDTH]))
    v0, v1 = both_halves(_dot(xb, wv_ref[...]))
    kv_ref[:, 0 * LANES:1 * LANES] = k0.astype(BF16)
    kv_ref[:, 1 * LANES:2 * LANES] = k1.astype(BF16)
    kv_ref[:, 2 * LANES:3 * LANES] = v0.astype(BF16)
    kv_ref[:, 3 * LANES:4 * LANES] = v1.astype(BF16)
    z_ref[...] = _dot(xb, wz_ref[...]).astype(BF16)
    xbc_ref[...] = _dot(xb, wx_ref[...]).astype(BF16)
    dt_ref[...] = _dot(xb, wdt_ref[...])


def _inproj(x2d, seq, apply_ln, lng, lnb, wqk, wv, wz, wx, wdt, cos_t, sin_t):
    t = x2d.shape[0]
    tm = TOKEN_TILE
    assert t % tm == 0 and seq % tm == 0
    tiles_per_seq = seq // tm
    row = lambda w: pl.BlockSpec((tm, w), lambda i: (i, 0))
    full = lambda a: pl.BlockSpec(a.shape, lambda i: (0,) * a.ndim)
    tab = pl.BlockSpec((tm, LANES), lambda i: (i % tiles_per_seq, 0))
    out_shape = [jax.ShapeDtypeStruct((t, ATTN_WIDTH), BF16), jax.ShapeDtypeStruct((t, 4 * LANES), BF16),
                 jax.ShapeDtypeStruct((t, SSD_WIDTH), BF16), jax.ShapeDtypeStruct((t, XBC_WIDTH), BF16),
                 jax.ShapeDtypeStruct((t, SSD_GROUPS * LANES), F32)]
    out_specs = [row(ATTN_WIDTH), row(4 * LANES), row(SSD_WIDTH), row(XBC_WIDTH), row(SSD_GROUPS * LANES)]
    if apply_ln:
        out_shape = [jax.ShapeDtypeStruct((t, D_MODEL), F32)] + out_shape
        out_specs = [row(D_MODEL)] + out_specs
    return pl.pallas_call(
        functools.partial(_inproj_kernel, apply_ln),
        grid=(t // tm,),
        in_specs=[row(D_MODEL), full(lng), full(lnb), full(wqk), full(wv), full(wz), full(wx), full(wdt), tab, tab],
        out_specs=out_specs,
        out_shape=out_shape,
        compiler_params=pltpu.CompilerParams(dimension_semantics=("arbitrary",), vmem_limit_bytes=VMEM_LIMIT),
        name="in_proj_ln" if apply_ln else "in_proj",
    )(x2d, lng, lnb, wqk, wv, wz, wx, wdt, cos_t, sin_t)


def _attn_kernel(seq, sink_ref, q_ref, kl_ref, km_ref, kr_ref, vl_ref, vm_ref, vr_ref, o_ref):
    bq = q_ref.shape[1]
    sub = WINDOW
    ks = sub + 2 * WINDOW
    start = pl.program_id(1) * bq
    qi = lax.broadcasted_iota(jnp.int32, (sub, ks), 0)
    kc = lax.broadcasted_iota(jnp.int32, (sub, ks), 1) - WINDOW
    band = jnp.abs(kc - qi) <= WINDOW
    lane = lax.broadcasted_iota(jnp.int32, (sub, LANES), 1)
    low_head = lane < HEAD_DIM
    group = N_Q_HEADS // N_KV_HEADS
    for h in range(N_KV_HEADS):
        ksl = slice(h * LANES, (h + 1) * LANES)
        kk = jnp.concatenate([kl_ref[0, :, ksl], km_ref[0, :, ksl], kr_ref[0, :, ksl]], axis=0)
        vv = jnp.concatenate([vl_ref[0, :, ksl], vm_ref[0, :, ksl], vr_ref[0, :, ksl]], axis=0)
        for sb in range(bq // sub):
            rows = slice(sb * sub, (sb + 1) * sub)
            kpos = kc + (start + sb * sub)
            allow = band & (kpos >= 0) & (kpos < seq)
            qs = []
            for r in range(group):
                head = group * h + r
                qt = q_ref[0, rows, (head // 2) * LANES:(head // 2 + 1) * LANES]
                keep = low_head if head % 2 == 0 else jnp.logical_not(low_head)
                qs.append(jnp.where(keep, qt, jnp.zeros_like(qt)))
            sc = _dot_nt(jnp.concatenate(qs, axis=0), kk[sb * sub:sb * sub + ks])
            ps, inv = [], []
            for r in range(group):
                sink = sink_ref[group * h + r]
                s = jnp.where(allow, sc[r * sub:(r + 1) * sub], MASK_VALUE)
                m = jnp.maximum(jnp.max(s, axis=-1, keepdims=True), sink)
                p = jnp.exp(s - m)
                denom = jnp.sum(p, axis=-1, keepdims=True) + jnp.exp(sink - m)
                ps.append(p.astype(BF16))
                inv.append(1.0 / denom)
            pv = _dot(jnp.concatenate(ps, axis=0), vv[sb * sub:sb * sub + ks])
            for pair in range(group // 2):
                even = pv[(2 * pair) * sub:(2 * pair + 1) * sub] * inv[2 * pair]
                odd = pv[(2 * pair + 1) * sub:(2 * pair + 2) * sub] * inv[2 * pair + 1]
                col = (group * h) // 2 + pair
                o_ref[0, rows, col * LANES:(col + 1) * LANES] = jnp.where(low_head, even, odd).astype(BF16)


def _attention(q, kv, sink):
    b, seq, _ = q.shape
    bq = ATTN_BQ
    assert seq % bq == 0 and bq % WINDOW == 0
    r = bq // WINDOW
    nwb = seq // WINDOW
    side = lambda lane_blk, f: pl.BlockSpec((1, WINDOW, 2 * LANES), lambda bi, i: (bi, f(i), lane_blk))
    mid = lambda lane_blk: pl.BlockSpec((1, bq, 2 * LANES), lambda bi, i: (bi, i, lane_blk))
    left = lambda i: jnp.maximum(i * r - 1, 0)
    right = lambda i: jnp.minimum((i + 1) * r, nwb - 1)
    return pl.pallas_call(
        functools.partial(_attn_kernel, seq),
        grid=(b, seq // bq),
        in_specs=[pl.BlockSpec(memory_space=pltpu.SMEM),
                  pl.BlockSpec((1, bq, ATTN_WIDTH), lambda bi, i: (bi, i, 0)),
                  side(0, left), mid(0), side(0, right), side(1, left), mid(1), side(1, right)],
        out_specs=pl.BlockSpec((1, bq, ATTN_WIDTH), lambda bi, i: (bi, i, 0)),
        out_shape=jax.ShapeDtypeStruct((b, seq, ATTN_WIDTH), BF16),
        compiler_params=pltpu.CompilerParams(dimension_semantics=("arbitrary", "arbitrary"),
                                             vmem_limit_bytes=VMEM_LIMIT),
        name="window_attn",
    )(sink, q, kv, kv, kv, kv, kv, kv)


def _ssd_kernel(x_ref, bm_ref, cm_ref, z_ref, dt_ref, cwx_ref, cwb_ref, cwc_ref, cbx_ref, cbb_ref, cbc_ref,
                gp_ref, dexp_ref, ng_ref, tril_ref, triu_ref, ef_ref, eb_ref, sh_ref, y_ref, ss_ref,
                pad_sc, xs_sc, bs_sc, cs_sc, bt_sc, yacc_sc, locb_sc, ef_sc, er_sc, keepf_sc, keepb_sc, fr_sc, rows_sc,
                sf_sc, sb_sc):
    seq = x_ref.shape[1]
    nchunks = seq // CHUNK
    gw = GROUP_WIDTH
    hp = HEADS_PER_GROUP
    pad = CONV_PAD
    taps = [k for k in range(SSD_CONV) if k != SSD_CONV // 2]

    zeros = jnp.zeros((pad, gw + 2 * LANES), BF16)
    pad_sc[0:pad, :] = zeros
    pad_sc[pad + seq:pad + seq + pad, :] = zeros

    def fill(c, carry):
        r0 = pl.multiple_of(c * CHUNK, CHUNK)
        pad_sc[pl.ds(pad + r0, CHUNK), 0:gw] = x_ref[0, pl.ds(r0, CHUNK), :]
        pad_sc[pl.ds(pad + r0, CHUNK), gw:gw + LANES] = bm_ref[0, pl.ds(r0, CHUNK), :]
        pad_sc[pl.ds(pad + r0, CHUNK), gw + LANES:gw + 2 * LANES] = cm_ref[0, pl.ds(r0, CHUNK), :]
        return carry

    lax.fori_loop(0, nchunks, fill, 0)

    lane_row = lax.broadcasted_iota(jnp.int32, (1, LANES), 1)
    dt_bias = gp_ref[0, 0:1, :]
    a_row = jnp.where(lane_row < 2 * hp, -jnp.exp(gp_ref[0, 1:2, :]), 0.0)
    dexp = dexp_ref[...]
    tril = tril_ref[...]
    triu = triu_ref[...]
    qi = lax.broadcasted_iota(jnp.int32, (CHUNK, CHUNK), 0)
    si = lax.broadcasted_iota(jnp.int32, (CHUNK, CHUNK), 1)
    lower = si <= qi
    before = si < qi
    after = si > qi
    low_head = si < SSD_HEAD_DIM
    fwd_lane = si < hp

    def blockdiag(t):
        return jnp.concatenate([jnp.where(low_head, t, 0.0), jnp.where(low_head, 0.0, t)], axis=0).astype(BF16)

    def head_rows(v, e_ref):
        return _dot(_split3(jnp.broadcast_to(v, (8, LANES))), e_ref[...])

    def chunk_scalars(r0):
        x = dt_ref[0, pl.ds(r0, CHUNK), :] + dt_bias
        dt = jnp.maximum(x, 0.0) + jnp.log1p(jnp.exp(-jnp.abs(x)))
        pieces = _split3(dt * a_row)
        fp = _dot(tril, pieces)
        rp = _dot(triu, pieces)
        f = fp[:, 0:LANES] + fp[:, LANES:2 * LANES] + fp[:, 2 * LANES:3 * LANES]
        r = rp[:, 0:LANES] + rp[:, LANES:2 * LANES] + rp[:, 2 * LANES:3 * LANES]
        return dt, f, r

    def expand(v, first):
        tiles = [jnp.where(low_head, v[:, first + 2 * p:first + 2 * p + 1], v[:, first + 2 * p + 1:first + 2 * p + 2])
                 for p in range(hp // 2)]
        return jnp.concatenate(tiles, axis=1)

    def conv(c, carry):
        r0 = pl.multiple_of(c * CHUNK, CHUNK)
        shifted = _dot(sh_ref[...], pad_sc[pl.ds(r0, CHUNK + 2 * pad), :])
        centre = pad_sc[pl.ds(r0 + pad, CHUNK), :].astype(F32)

        def one(lo, hi, w_ref, b_ref):
            acc = b_ref[...] + w_ref[SSD_CONV // 2:SSD_CONV // 2 + 1, :] * centre[:, lo:hi]
            for j, k in enumerate(taps):
                acc = acc + w_ref[k:k + 1, :] * shifted[j * CHUNK:(j + 1) * CHUNK, lo:hi]
            return _silu(acc)

        xs_sc[pl.ds(r0, CHUNK), :] = one(0, gw, cwx_ref, cbx_ref)
        bconv = one(gw, gw + LANES, cwb_ref, cbb_ref)
        bs_sc[pl.ds(r0, CHUNK), :] = bconv.astype(BF16)
        bt_sc[c] = bconv.T
        cs_sc[pl.ds(r0, CHUNK), :] = one(gw + LANES, gw + 2 * LANES, cwc_ref, cbc_ref).astype(BF16)

        dt, f, r = chunk_scalars(r0)
        fr = jnp.where(fwd_lane, f, r)
        fr_sc[c] = fr * LOG2E
        pt = jnp.where(si < 2 * hp, fr, dt).T
        cums_t = pt[0:2 * hp, :]
        dt_t = pt[2 * hp:4 * hp, :]
        dte_t = jnp.exp(jnp.where(qi[0:2 * hp] < hp, cums_t[:, CHUNK - 1:CHUNK] - cums_t, cums_t[:, 0:1] - cums_t)) * dt_t
        rows_sc[c, 0:2 * hp, :] = cums_t * LOG2E
        rows_sc[c, 2 * hp:4 * hp, :] = dt_t
        rows_sc[c, 4 * hp:6 * hp, :] = dte_t
        keepf_sc[c] = head_rows(jnp.exp(f[CHUNK - 1:CHUNK, :]), ef_ref)
        keepb_sc[c] = head_rows(jnp.exp(r[0:1, :]), eb_ref)
        ef_sc[c] = expand(jnp.exp(f), 0)
        er_sc[c] = expand(jnp.exp(r), hp)
        return carry

    lax.fori_loop(0, nchunks, conv, 0, unroll=4)

    sf_sc[...] = jnp.zeros_like(sf_sc)
    sb_sc[...] = jnp.zeros_like(sb_sc)

    def ascend(c, carry):
        r0 = pl.multiple_of(c * CHUNK, CHUNK)
        fr = fr_sc[c]
        cums_t = rows_sc[c, 0:2 * hp, :]
        dt_t = rows_sc[c, 2 * hp:4 * hp, :]
        dte_t = rows_sc[c, 4 * hp:6 * hp, :]
        keep_f = keepf_sc[c]
        xs = xs_sc[pl.ds(r0, CHUNK), :]
        cc = cs_sc[pl.ds(r0, CHUNK), :]
        cb = _dot_nt(cc, bs_sc[pl.ds(r0, CHUNK), :])
        yoff = _dot(cc, sf_sc[...].astype(BF16)) * ef_sc[c]
        bt = bt_sc[c]
        for pair in range(hp // 2):
            ms, bfs, bbs = [], [], []
            for i in (2 * pair, 2 * pair + 1):
                seg = jnp.where(lower, fr[:, i:i + 1] - cums_t[i:i + 1, :], fr[:, hp + i:hp + i + 1] - cums_t[hp + i:hp + i + 1, :])
                dtf_row = dt_t[i:i + 1, :]
                dtb_row = dt_t[hp + i:hp + i + 1, :]
                w = jnp.where(before, dtf_row, jnp.where(after, dtb_row, dtf_row + dtb_row))
                ms.append((cb * jnp.exp2(seg) * w).astype(BF16))
                bfs.append((bt * dte_t[i:i + 1, :]).astype(BF16))
                bbs.append((bt * dte_t[hp + i:hp + i + 1, :]).astype(BF16))
            sl = slice(pair * LANES, (pair + 1) * LANES)
            lhs = jnp.concatenate([jnp.concatenate(ms, axis=1), jnp.concatenate(bfs, axis=1),
                                   jnp.concatenate(bbs, axis=1)], axis=0)
            res = _dot(lhs, blockdiag(xs[:, sl]))
            yacc_sc[pl.ds(r0, CHUNK), sl] = res[0:CHUNK] + yoff[:, sl] + xs[:, sl] * dexp[:, sl]
            sf_sc[:, sl] = sf_sc[:, sl] * keep_f[0:1, sl] + res[CHUNK:2 * CHUNK]
            locb_sc[c, :, sl] = res[2 * CHUNK:3 * CHUNK]
        return carry

    lax.fori_loop(0, nchunks, ascend, 0, unroll=2)

    def descend(j, carry):
        c = nchunks - 1 - j
        r0 = pl.multiple_of(c * CHUNK, CHUNK)
        sb = sb_sc[...]
        y = yacc_sc[pl.ds(r0, CHUNK), :] + _dot(cs_sc[pl.ds(r0, CHUNK), :], sb.astype(BF16)) * er_sc[c]
        sb_sc[...] = sb * keepb_sc[c][0:1, :] + locb_sc[c]
        gated = y * _silu(z_ref[0, pl.ds(r0, CHUNK), :].astype(F32))
        ss = jnp.sum(gated * gated, axis=1, keepdims=True)
        ss_ref[0, pl.ds(r0, CHUNK), :] = jnp.broadcast_to(ss, (CHUNK, LANES))
        y_ref[0, pl.ds(r0, CHUNK), :] = (gated * ng_ref[...]).astype(BF16)
        return carry

    lax.fori_loop(0, nchunks, descend, 0, unroll=4)


def _ssd_constants():
    idx = np.arange(CHUNK)
    tril = (idx[None, :] <= idx[:, None]).astype(np.float32)
    triu = (idx[None, :] >= idx[:, None]).astype(np.float32)
    rows = np.arange(3 * LANES)[:, None] % LANES
    cols = np.arange(GROUP_WIDTH)[None, :] // SSD_HEAD_DIM
    ef = (rows == cols).astype(np.float32)
    eb = (rows == cols + HEADS_PER_GROUP).astype(np.float32)
    taps = [k for k in range(SSD_CONV) if k != SSD_CONV // 2]
    src = np.arange(CHUNK + 2 * CONV_PAD)[None, :]
    shift = np.concatenate([(src == idx[:, None] + CONV_PAD + k - SSD_CONV // 2) for k in taps], axis=0).astype(np.float32)
    return tuple(jnp.asarray(a, dtype=BF16) for a in (tril, triu, ef, eb, shift))


def _ssd(xbc, z, dt, conv_w, conv_b, gp, d_exp, norm_g):
    b, seq, _ = xbc.shape
    assert seq % CHUNK == 0
    gw = GROUP_WIDTH
    nx = SSD_WIDTH // LANES
    nchunks = seq // CHUNK
    consts = _ssd_constants()
    seq_blk = lambda w, f: pl.BlockSpec((1, seq, w), lambda bi, g: (bi, 0, f(g)))
    par = lambda rows, w, f: pl.BlockSpec((rows, w), lambda bi, g: (0, f(g)))
    full = lambda a: pl.BlockSpec(a.shape, lambda bi, g: (0,) * a.ndim)
    return pl.pallas_call(
        _ssd_kernel,
        grid=(b, SSD_GROUPS),
        in_specs=[seq_blk(gw, lambda g: g), seq_blk(LANES, lambda g: nx + g), seq_blk(LANES, lambda g: nx + SSD_GROUPS + g),
                  seq_blk(gw, lambda g: g), seq_blk(LANES, lambda g: g),
                  par(SSD_CONV, gw, lambda g: g), par(SSD_CONV, LANES, lambda g: nx + g),
                  par(SSD_CONV, LANES, lambda g: nx + SSD_GROUPS + g),
                  par(1, gw, lambda g: g), par(1, LANES, lambda g: nx + g), par(1, LANES, lambda g: nx + SSD_GROUPS + g),
                  pl.BlockSpec((1, 8, LANES), lambda bi, g: (g, 0, 0)), par(1, gw, lambda g: g), par(1, gw, lambda g: g)]
                 + [full(a) for a in consts],
        out_specs=[seq_blk(gw, lambda g: g), seq_blk(LANES, lambda g: g)],
        out_shape=[jax.ShapeDtypeStruct((b, seq, SSD_WIDTH), BF16), jax.ShapeDtypeStruct((b, seq, SSD_GROUPS * LANES), F32)],
        scratch_shapes=[pltpu.VMEM((seq + 2 * CONV_PAD, gw + 2 * LANES), BF16), pltpu.VMEM((seq, gw), F32),
                        pltpu.VMEM((seq, LANES), BF16), pltpu.VMEM((seq, LANES), BF16),
                        pltpu.VMEM((nchunks, SSD_STATE, CHUNK), F32), pltpu.VMEM((seq, gw), F32),
                        pltpu.VMEM((nchunks, SSD_STATE, gw), F32), pltpu.VMEM((nchunks, CHUNK, gw), F32),
                        pltpu.VMEM((nchunks, CHUNK, gw), F32), pltpu.VMEM((nchunks, 8, gw), F32), pltpu.VMEM((nchunks, 8, gw), F32),
                        pltpu.VMEM((nchunks, CHUNK, LANES), F32), pltpu.VMEM((nchunks, 6 * HEADS_PER_GROUP, LANES), F32),
                        pltpu.VMEM((SSD_STATE, gw), F32), pltpu.VMEM((SSD_STATE, gw), F32)],
        compiler_params=pltpu.CompilerParams(dimension_semantics=("arbitrary", "arbitrary"),
                                             vmem_limit_bytes=VMEM_LIMIT),
        name="bidir_ssd",
    )(xbc, xbc, xbc, z, dt, conv_w, conv_w, conv_w, conv_b, conv_b, conv_b, gp, d_exp, norm_g, *consts)


def _outproj_kernel(attn_ref, y_ref, ss_ref, h_ref, wa_ref, ws_ref, g_ref, b_ref, o_ref):
    ss = ss_ref[...]
    total = ss[:, 0:1]
    for g in range(1, SSD_GROUPS):
        total = total + ss[:, g * LANES:g * LANES + 1]
    rs = lax.rsqrt(total * (1.0 / SSD_WIDTH) + EPS)
    mix = _dot(attn_ref[...], wa_ref[...]) + rs * _dot(y_ref[...], ws_ref[...])
    o_ref[...] = _layer_norm(DEEPNORM_ALPHA * h_ref[...] + mix, g_ref[...], b_ref[...])


def _outproj(attn, y, ss, h, wa, ws, g, b):
    t = h.shape[0]
    tm = TOKEN_TILE
    row = lambda w: pl.BlockSpec((tm, w), lambda i: (i, 0))
    full = lambda a: pl.BlockSpec(a.shape, lambda i: (0,) * a.ndim)
    return pl.pallas_call(
        _outproj_kernel,
        grid=(t // tm,),
        in_specs=[row(ATTN_WIDTH), row(SSD_WIDTH), row(SSD_GROUPS * LANES), row(D_MODEL), full(wa), full(ws), full(g), full(b)],
        out_specs=row(D_MODEL),
        out_shape=jax.ShapeDtypeStruct((t, D_MODEL), F32),
        compiler_params=pltpu.CompilerParams(dimension_semantics=("arbitrary",), vmem_limit_bytes=VMEM_LIMIT),
        name="out_proj_ln",
    )(attn, y, ss, h, wa, ws, g, b)


def _ffn_kernel(tiles_per_seq, hp_ref, h_ref, hn_ref, wup_ref, cw_ref, cb_ref, wdn_ref, g_ref, b_ref, o_ref,
                hb_sc, act_sc):
    tm = h_ref.shape[0]
    pos = pl.program_id(0) % tiles_per_seq
    h = h_ref[...]
    hb_sc[0:FF_HALO, :] = jnp.where(pos == 0, 0.0, hp_ref[...]).astype(BF16)
    hb_sc[FF_HALO:FF_HALO + tm, :] = h.astype(BF16)
    hb_sc[FF_HALO + tm:, :] = jnp.where(pos == tiles_per_seq - 1, 0.0, hn_ref[...]).astype(BF16)
    for j in range(D_FF // FF_TILE):
        cols = slice(j * FF_TILE, (j + 1) * FF_TILE)
        vcols = slice(D_FF + j * FF_TILE, D_FF + (j + 1) * FF_TILE)
        g_ext = _dot(hb_sc[...], wup_ref[:, cols])
        val = _dot(hb_sc[FF_HALO:FF_HALO + tm, :], wup_ref[:, vcols])
        gate = cb_ref[:, cols] + cw_ref[1:2, cols] * g_ext[FF_HALO:FF_HALO + tm]
        gate = gate + cw_ref[0:1, cols] * g_ext[FF_HALO - 1:FF_HALO - 1 + tm]
        gate = gate + cw_ref[2:3, cols] * g_ext[FF_HALO + 1:FF_HALO + 1 + tm]
        act_sc[:, cols] = (_silu(gate) * val).astype(BF16)
    slab = tm // FFN_OUT_SLABS
    for s in range(FFN_OUT_SLABS):
        rows = slice(s * slab, (s + 1) * slab)
        ffn = _dot(act_sc[rows, :], wdn_ref[...])
        o_ref[rows, :] = _layer_norm(DEEPNORM_ALPHA * h_ref[rows, :] + ffn, g_ref[...], b_ref[...])


def _ffn(h, seq, wup, cw, cb, wdn, g, b):
    t = h.shape[0]
    tm = FFN_TOKEN_TILE
    assert seq % tm == 0 and tm % FF_HALO == 0 and D_FF % FF_TILE == 0
    tiles_per_seq = seq // tm
    hb = tm // FF_HALO
    nhb = t // FF_HALO
    row = pl.BlockSpec((tm, D_MODEL), lambda i: (i, 0))
    prev = pl.BlockSpec((FF_HALO, D_MODEL), lambda i: (jnp.maximum(i * hb - 1, 0), 0))
    nxt = pl.BlockSpec((FF_HALO, D_MODEL), lambda i: (jnp.minimum((i + 1) * hb, nhb - 1), 0))
    full = lambda a: pl.BlockSpec(a.shape, lambda i: (0,) * a.ndim, pipeline_mode=pl.Buffered(1))
    return pl.pallas_call(
        functools.partial(_ffn_kernel, tiles_per_seq),
        grid=(t // tm,),
        in_specs=[prev, row, nxt, full(wup), full(cw), full(cb), full(wdn), full(g), full(b)],
        out_specs=row,
        out_shape=jax.ShapeDtypeStruct((t, D_MODEL), F32),
        scratch_shapes=[pltpu.VMEM((tm + 2 * FF_HALO, D_MODEL), BF16), pltpu.VMEM((tm, D_FF), BF16)],
        compiler_params=pltpu.CompilerParams(dimension_semantics=("arbitrary",), vmem_limit_bytes=VMEM_LIMIT),
        name="ffn_ln",
    )(h, h, h, wup, cw, cb, wdn, g, b)


def _rope_tables(seq):
    half = HEAD_DIM // 2
    inv_freq = ROPE_THETA ** (-jnp.arange(0, HEAD_DIM, 2, dtype=F32) / HEAD_DIM)
    ang = jnp.arange(seq, dtype=F32)[:, None] * inv_freq[None, :]
    cos, sin = jnp.cos(ang), jnp.sin(ang)
    reps = LANES // HEAD_DIM
    return jnp.tile(jnp.concatenate([cos, cos], axis=1), (1, reps)), jnp.tile(jnp.concatenate([-sin, sin], axis=1), (1, reps))


def _per_group(fwd, bwd, offset_f, offset_b):
    out = jnp.zeros((SSD_GROUPS, LANES), F32)
    for g in range(SSD_GROUPS):
        sl = slice(g * HEADS_PER_GROUP, (g + 1) * HEADS_PER_GROUP)
        if fwd is not None:
            out = out.at[g, offset_f:offset_f + HEADS_PER_GROUP].set(fwd[sl])
        if bwd is not None:
            out = out.at[g, offset_b:offset_b + HEADS_PER_GROUP].set(bwd[sl])
    return out


def _layer_params(w_in, ssd_a_log, ssd_dt_bias, ssd_d, w_out, w_up, w_down):
    o_k = ATTN_WIDTH + KV_WIDTH
    o_z = o_k + KV_WIDTH
    o_x = o_z + SSD_WIDTH
    o_dt = o_x + XBC_WIDTH
    hp = HEADS_PER_GROUP
    wdt_src = w_in[:, o_dt:]
    wdt = jnp.zeros((D_MODEL, SSD_GROUPS * LANES), F32)
    for g in range(SSD_GROUPS):
        wdt = wdt.at[:, g * LANES:g * LANES + hp].set(wdt_src[:, g * hp:(g + 1) * hp])
        wdt = wdt.at[:, g * LANES + hp:g * LANES + 2 * hp].set(wdt_src[:, SSD_HEADS + g * hp:SSD_HEADS + (g + 1) * hp])
        wdt = wdt.at[:, g * LANES + 2 * hp:g * LANES + 4 * hp].set(wdt[:, g * LANES:g * LANES + 2 * hp])
    gp = jnp.zeros((SSD_GROUPS, 8, LANES), F32)
    dt_bias = _per_group(ssd_dt_bias[0], ssd_dt_bias[1], 0, hp)
    gp = gp.at[:, 0, :].set(dt_bias.at[:, 2 * hp:4 * hp].set(dt_bias[:, 0:2 * hp]))
    gp = gp.at[:, 1, :].set(_per_group(ssd_a_log[0], ssd_a_log[1], 0, hp))
    return dict(
        wqk=w_in[:, :o_k].astype(BF16), wv=w_in[:, o_k:o_z].astype(BF16), wz=w_in[:, o_z:o_x].astype(BF16),
        wx=w_in[:, o_x:o_dt].astype(BF16), wdt=wdt.astype(BF16), gp=gp,
        d_exp=jnp.repeat(ssd_d, SSD_HEAD_DIM)[None, :],
        wa=w_out[:ATTN_WIDTH].astype(BF16), ws=w_out[ATTN_WIDTH:].astype(BF16),
        wup=w_up.astype(BF16), wdn=w_down.astype(BF16))


def _trunk(x, ln_in_g, ln_in_b, layers):
    b, seq, _ = x.shape
    cos_t, sin_t = _rope_tables(seq)
    h = x.reshape(b * seq, D_MODEL)
    lng, lnb = ln_in_g[None, :], ln_in_b[None, :]
    for li, p in enumerate(layers):
        outs = _inproj(h, seq, li == 0, lng, lnb, p["wqk"], p["wv"], p["wz"], p["wx"], p["wdt"], cos_t, sin_t)
        if li == 0:
            h, outs = outs[0], outs[1:]
        q, kv, z, xbc, dt = outs
        three = lambda a: a.reshape(b, seq, a.shape[-1])
        attn = _attention(three(q), three(kv), p["sink"])
        y, ss = _ssd(three(xbc), three(z), three(dt), p["conv_w"], p["conv_b"], p["gp"], p["d_exp"], p["norm_g"])
        h = _outproj(attn.reshape(b * seq, -1), y.reshape(b * seq, -1), ss.reshape(b * seq, -1), h,
                     p["wa"], p["ws"], p["ln1_g"], p["ln1_b"])
        h = _ffn(h, seq, p["wup"], p["ffn_cw"], p["ffn_cb"], p["wdn"], p["ln2_g"], p["ln2_b"])
    return h.reshape(b, seq, D_MODEL)


def kernel(x_prompt, x_sample, ln_in_g, ln_in_b, w_in, attn_sink, ssd_conv_w, ssd_conv_b, ssd_a_log, ssd_dt_bias, ssd_d, ssd_norm_g, w_out, ln1_g, ln1_b, w_up, ffn_conv_w, ffn_conv_b, w_down, ln2_g, ln2_b):
    layers = []
    for i in range(w_in.shape[0]):
        p = _layer_params(w_in[i], ssd_a_log[i], ssd_dt_bias[i], ssd_d[i], w_out[i], w_up[i], w_down[i])
        p.update(sink=attn_sink[i], conv_w=ssd_conv_w[i], conv_b=ssd_conv_b[i][None, :], norm_g=ssd_norm_g[i][None, :],
                 ln1_g=ln1_g[i][None, :], ln1_b=ln1_b[i][None, :], ffn_cw=ffn_conv_w[i], ffn_cb=ffn_conv_b[i][None, :],
                 ln2_g=ln2_g[i][None, :], ln2_b=ln2_b[i][None, :])
        layers.append(p)
    return (_trunk(x_prompt, ln_in_g, ln_in_b, layers), _trunk(x_sample, ln_in_g, ln_in_b, layers))
```

```python
import functools
import math

import jax
import jax.numpy as jnp
import numpy as np
from jax import lax
from jax.experimental import pallas as pl
from jax.experimental.pallas import tpu as pltpu

F32 = jnp.float32
BF16 = jnp.bfloat16

D_MODEL = 1024
DEPTH = 4
HEAD_DIM = 64
N_Q_HEADS = 8
N_KV_HEADS = 2
ATTN_WIDTH = N_Q_HEADS * HEAD_DIM
KV_WIDTH = N_KV_HEADS * HEAD_DIM
WINDOW = 128
ROPE_THETA = 10000.0
SSD_HEADS = 16
SSD_HEAD_DIM = 64
SSD_WIDTH = SSD_HEADS * SSD_HEAD_DIM
SSD_GROUPS = 2
HEADS_PER_GROUP = SSD_HEADS // SSD_GROUPS
GROUP_WIDTH = HEADS_PER_GROUP * SSD_HEAD_DIM
SSD_STATE = 128
SSD_CONV = 5
CHUNK = 128
XBC_WIDTH = SSD_WIDTH + 2 * SSD_GROUPS * SSD_STATE
D_FF = 2816
FFN_CONV = 3
DEEPNORM_ALPHA = (2 * DEPTH) ** 0.25
EPS = 1e-5
MASK_VALUE = -1e30
LOG2E = math.log2(math.e)

LANES = 128
VMEM_LIMIT = 56 * 1024 * 1024
TOKEN_TILE = 1024
FFN_TOKEN_TILE = 1024
ATTN_BQ = 512
FF_TILE = 256
FFN_OUT_SLABS = 4
FF_HALO = 16
CONV_PAD = 16


def _dot(a, b):
    return jnp.dot(a, b, preferred_element_type=F32)


def _dot_nt(a, b):
    return lax.dot_general(a, b, (((1,), (1,)), ((), ())), preferred_element_type=F32)


def _layer_norm(x, g, b):
    mu = jnp.mean(x, axis=-1, keepdims=True)
    xc = x - mu
    var = jnp.mean(xc * xc, axis=-1, keepdims=True)
    return xc * lax.rsqrt(var + EPS) * g + b


def _silu(x):
    return x * (1.0 / (1.0 + jnp.exp(-x)))


def _split3(x):
    hi = x.astype(BF16)
    r1 = x - hi.astype(F32)
    mid = r1.astype(BF16)
    lo = (r1 - mid.astype(F32)).astype(BF16)
    return jnp.concatenate([hi, mid, lo], axis=1)


def _inproj_kernel(apply_ln, x_ref, lng_ref, lnb_ref, wqk_ref, wv_ref, wz_ref, wx_ref, wdt_ref,
                   cos_ref, sin_ref, *outs):
    if apply_ln:
        h_ref, q_ref, kv_ref, z_ref, xbc_ref, dt_ref = outs
    else:
        q_ref, kv_ref, z_ref, xbc_ref, dt_ref = outs
    x = x_ref[...]
    if apply_ln:
        x = _layer_norm(x, lng_ref[...], lnb_ref[...])
        h_ref[...] = x
    xb = x.astype(BF16)
    tm = x.shape[0]
    cos = cos_ref[...]
    sin = sin_ref[...]
    lane = lax.broadcasted_iota(jnp.int32, (tm, LANES), 1)
    first_half = (lane % HEAD_DIM) < (HEAD_DIM // 2)
    low_head = lane < HEAD_DIM

    def rope(t):
        swapped = jnp.where(first_half, pltpu.roll(t, LANES - HEAD_DIM // 2, 1), pltpu.roll(t, HEAD_DIM // 2, 1))
        return t * cos + swapped * sin

    def both_halves(t):
        sw = pltpu.roll(t, HEAD_DIM, 1)
        return jnp.where(low_head, t, sw), jnp.where(low_head, sw, t)

    qk = _dot(xb, wqk_ref[...])
    scale = HEAD_DIM ** -0.5 * LOG2E
    for j in range(ATTN_WIDTH // LANES):
        q_ref[:, j * LANES:(j + 1) * LANES] = (rope(qk[:, j * LANES:(j + 1) * LANES]) * scale).astype(BF16)
    k0, k1 = both_halves(rope(qk[:, ATTN_WIDTH:ATTN_WIDTH + KV_WIDTH]))
    v0, v1 = both_halves(_dot(xb, wv_ref[...]))
    kv_ref[:, 0 * LANES:1 * LANES] = k0.astype(BF16)
    kv_ref[:, 1 * LANES:2 * LANES] = k1.astype(BF16)
    kv_ref[:, 2 * LANES:3 * LANES] = v0.astype(BF16)
    kv_ref[:, 3 * LANES:4 * LANES] = v1.astype(BF16)
    z_ref[...] = _dot(xb, wz_ref[...]).astype(BF16)
    xbc_ref[...] = _dot(xb, wx_ref[...]).astype(BF16)
    dt_ref[...] = _dot(xb, wdt_ref[...])


def _inproj(x2d, seq, apply_ln, lng, lnb, wqk, wv, wz, wx, wdt, cos_t, sin_t):
    t = x2d.shape[0]
    tm = TOKEN_TILE
    assert t % tm == 0 and seq % tm == 0
    tiles_per_seq = seq // tm
    row = lambda w: pl.BlockSpec((tm, w), lambda i: (i, 0))
    full = lambda a: pl.BlockSpec(a.shape, lambda i: (0,) * a.ndim)
    tab = pl.BlockSpec((tm, LANES), lambda i: (i % tiles_per_seq, 0))
    out_shape = [jax.ShapeDtypeStruct((t, ATTN_WIDTH), BF16), jax.ShapeDtypeStruct((t, 4 * LANES), BF16),
                 jax.ShapeDtypeStruct((t, SSD_WIDTH), BF16), jax.ShapeDtypeStruct((t, XBC_WIDTH), BF16),
                 jax.ShapeDtypeStruct((t, SSD_GROUPS * LANES), F32)]
    out_specs = [row(ATTN_WIDTH), row(4 * LANES), row(SSD_WIDTH), row(XBC_WIDTH), row(SSD_GROUPS * LANES)]
    if apply_ln:
        out_shape = [jax.ShapeDtypeStruct((t, D_MODEL), F32)] + out_shape
        out_specs = [row(D_MODEL)] + out_specs
    return pl.pallas_call(
        functools.partial(_inproj_kernel, apply_ln),
        grid=(t // tm,),
        in_specs=[row(D_MODEL), full(lng), full(lnb), full(wqk), full(wv), full(wz), full(wx), full(wdt), tab, tab],
        out_specs=out_specs,
        out_shape=out_shape,
        compiler_params=pltpu.CompilerParams(dimension_semantics=("arbitrary",), vmem_limit_bytes=VMEM_LIMIT),
        name="in_proj_ln" if apply_ln else "in_proj",
    )(x2d, lng, lnb, wqk, wv, wz, wx, wdt, cos_t, sin_t)


def _attn_kernel(seq, sink_ref, q_ref, kl_ref, km_ref, kr_ref, vl_ref, vm_ref, vr_ref, o_ref):
    bq = q_ref.shape[1]
    sub = WINDOW
    ks = sub + 2 * WINDOW
    start = pl.program_id(1) * bq
    qi = lax.broadcasted_iota(jnp.int32, (sub, ks), 0)
    kc = lax.broadcasted_iota(jnp.int32, (sub, ks), 1) - WINDOW
    band = jnp.abs(kc - qi) <= WINDOW
    lane = lax.broadcasted_iota(jnp.int32, (sub, LANES), 1)
    low_head = lane < HEAD_DIM
    group = N_Q_HEADS // N_KV_HEADS
    for h in range(N_KV_HEADS):
        ksl = slice(h * LANES, (h + 1) * LANES)
        kk = jnp.concatenate([kl_ref[0, :, ksl], km_ref[0, :, ksl], kr_ref[0, :, ksl]], axis=0)
        vv = jnp.concatenate([vl_ref[0, :, ksl], vm_ref[0, :, ksl], vr_ref[0, :, ksl]], axis=0)
        for sb in range(bq // sub):
            rows = slice(sb * sub, (sb + 1) * sub)
            kpos = kc + (start + sb * sub)
            allow = band & (kpos >= 0) & (kpos < seq)
            qs = []
            for r in range(group):
                head = group * h + r
                qt = q_ref[0, rows, (head // 2) * LANES:(head // 2 + 1) * LANES]
                keep = low_head if head % 2 == 0 else jnp.logical_not(low_head)
                qs.append(jnp.where(keep, qt, jnp.zeros_like(qt)))
            sc = _dot_nt(jnp.concatenate(qs, axis=0), kk[sb * sub:sb * sub + ks])
            ps, inv = [], []
            for r in range(group):
                sink = sink_ref[group * h + r] * LOG2E
                s = sc[r * sub:(r + 1) * sub]
                s = jnp.concatenate([jnp.where(allow[:, :WINDOW], s[:, :WINDOW], MASK_VALUE), s[:, WINDOW:ks - WINDOW],
                                     jnp.where(allow[:, ks - WINDOW:], s[:, ks - WINDOW:], MASK_VALUE)], axis=1)
                m = jnp.maximum(jnp.max(s, axis=-1, keepdims=True), sink)
                p = jnp.exp2(s - m)
                denom = jnp.sum(p, axis=-1, keepdims=True) + jnp.exp2(sink - m)
                ps.append(p.astype(BF16))
                inv.append(1.0 / denom)
            pv = _dot(jnp.concatenate(ps, axis=0), vv[sb * sub:sb * sub + ks])
            for pair in range(group // 2):
                even = pv[(2 * pair) * sub:(2 * pair + 1) * sub] * inv[2 * pair]
                odd = pv[(2 * pair + 1) * sub:(2 * pair + 2) * sub] * inv[2 * pair + 1]
                col = (group * h) // 2 + pair
                o_ref[0, rows, col * LANES:(col + 1) * LANES] = jnp.where(low_head, even, odd).astype(BF16)


def _attention(q, kv, sink):
    b, seq, _ = q.shape
    bq = ATTN_BQ
    assert seq % bq == 0 and bq % WINDOW == 0
    r = bq // WINDOW
    nwb = seq // WINDOW
    side = lambda lane_blk, f: pl.BlockSpec((1, WINDOW, 2 * LANES), lambda bi, i: (bi, f(i), lane_blk))
    mid = lambda lane_blk: pl.BlockSpec((1, bq, 2 * LANES), lambda bi, i: (bi, i, lane_blk))
    left = lambda i: jnp.maximum(i * r - 1, 0)
    right = lambda i: jnp.minimum((i + 1) * r, nwb - 1)
    return pl.pallas_call(
        functools.partial(_attn_kernel, seq),
        grid=(b, seq // bq),
        in_specs=[pl.BlockSpec(memory_space=pltpu.SMEM),
                  pl.BlockSpec((1, bq, ATTN_WIDTH), lambda bi, i: (bi, i, 0)),
                  side(0, left), mid(0), side(0, right), side(1, left), mid(1), side(1, right)],
        out_specs=pl.BlockSpec((1, bq, ATTN_WIDTH), lambda bi, i: (bi, i, 0)),
        out_shape=jax.ShapeDtypeStruct((b, seq, ATTN_WIDTH), BF16),
        compiler_params=pltpu.CompilerParams(dimension_semantics=("arbitrary", "arbitrary"),
                                             vmem_limit_bytes=VMEM_LIMIT),
        name="window_attn",
    )(sink, q, kv, kv, kv, kv, kv, kv)


def _ssd_kernel(x_ref, bm_ref, cm_ref, z_ref, dt_ref, cwx_ref, cwb_ref, cwc_ref, cbx_ref, cbb_ref, cbc_ref,
                gp_ref, dexp_ref, ng_ref, tril_ref, triu_ref, ef_ref, eb_ref, sh_ref, y_ref, ss_ref,
                pad_sc, xs_sc, bs_sc, cs_sc, bt_sc, yacc_sc, locb_sc, ef_sc, er_sc, keepf_sc, keepb_sc, fr_sc, rows_sc,
                sf_sc, sb_sc):
    seq = x_ref.shape[1]
    nchunks = seq // CHUNK
    gw = GROUP_WIDTH
    hp = HEADS_PER_GROUP
    pad = CONV_PAD
    taps = [k for k in range(SSD_CONV) if k != SSD_CONV // 2]

    zeros = jnp.zeros((pad, gw + 2 * LANES), BF16)
    pad_sc[0:pad, :] = zeros
    pad_sc[pad + seq:pad + seq + pad, :] = zeros

    def fill(c, carry):
        r0 = pl.multiple_of(c * CHUNK, CHUNK)
        pad_sc[pl.ds(pad + r0, CHUNK), 0:gw] = x_ref[0, pl.ds(r0, CHUNK), :]
        pad_sc[pl.ds(pad + r0, CHUNK), gw:gw + LANES] = bm_ref[0, pl.ds(r0, CHUNK), :]
        pad_sc[pl.ds(pad + r0, CHUNK), gw + LANES:gw + 2 * LANES] = cm_ref[0, pl.ds(r0, CHUNK), :]
        return carry

    lax.fori_loop(0, nchunks, fill, 0)

    lane_row = lax.broadcasted_iota(jnp.int32, (1, LANES), 1)
    dt_bias = gp_ref[0, 0:1, :]
    a_row = jnp.where(lane_row < 2 * hp, -jnp.exp(gp_ref[0, 1:2, :]), 0.0)
    dexp = dexp_ref[...]
    tril = tril_ref[...]
    triu = triu_ref[...]
    qi = lax.broadcasted_iota(jnp.int32, (CHUNK, CHUNK), 0)
    si = lax.broadcasted_iota(jnp.int32, (CHUNK, CHUNK), 1)
    lower = si <= qi
    before = si < qi
    after = si > qi
    low_head = si < SSD_HEAD_DIM
    fwd_lane = si < hp

    def blockdiag(t):
        return jnp.concatenate([jnp.where(low_head, t, 0.0), jnp.where(low_head, 0.0, t)], axis=0).astype(BF16)

    def head_rows(v, e_ref):
        return _dot(_split3(jnp.broadcast_to(v, (8, LANES))), e_ref[...])

    def chunk_scalars(r0):
        x = dt_ref[0, pl.ds(r0, CHUNK), :] + dt_bias
        dt = jnp.maximum(x, 0.0) + jnp.log1p(jnp.exp(-jnp.abs(x)))
        pieces = _split3(dt * a_row)
        fp = _dot(tril, pieces)
        rp = _dot(triu, pieces)
        f = fp[:, 0:LANES] + fp[:, LANES:2 * LANES] + fp[:, 2 * LANES:3 * LANES]
        r = rp[:, 0:LANES] + rp[:, LANES:2 * LANES] + rp[:, 2 * LANES:3 * LANES]
        return dt, f, r

    def expand(v, first):
        tiles = [jnp.where(low_head, v[:, first + 2 * p:first + 2 * p + 1], v[:, first + 2 * p + 1:first + 2 * p + 2])
                 for p in range(hp // 2)]
        return jnp.concatenate(tiles, axis=1)

    def conv(c, carry):
        r0 = pl.multiple_of(c * CHUNK, CHUNK)
        shifted = _dot(sh_ref[...], pad_sc[pl.ds(r0, CHUNK + 2 * pad), :])
        centre = pad_sc[pl.ds(r0 + pad, CHUNK), :].astype(F32)

        def one(lo, hi, w_ref, b_ref):
            acc = b_ref[...] + w_ref[SSD_CONV // 2:SSD_CONV // 2 + 1, :] * centre[:, lo:hi]
            for j, k in enumerate(taps):
                acc = acc + w_ref[k:k + 1, :] * shifted[j * CHUNK:(j + 1) * CHUNK, lo:hi]
            return _silu(acc)

        xs_sc[pl.ds(r0, CHUNK), :] = one(0, gw, cwx_ref, cbx_ref)
        bconv = one(gw, gw + LANES, cwb_ref, cbb_ref)
        bs_sc[pl.ds(r0, CHUNK), :] = bconv.astype(BF16)
        bt_sc[c] = bconv.T
        cs_sc[pl.ds(r0, CHUNK), :] = one(gw + LANES, gw + 2 * LANES, cwc_ref, cbc_ref).astype(BF16)

        dt, f, r = chunk_scalars(r0)
        fr = jnp.where(fwd_lane, f, r)
        fr_sc[c] = fr * LOG2E
        pt = jnp.where(si < 2 * hp, fr, dt).T
        cums_t = pt[0:2 * hp, :]
        dt_t = pt[2 * hp:4 * hp, :]
        dte_t = jnp.exp(jnp.where(qi[0:2 * hp] < hp, cums_t[:, CHUNK - 1:CHUNK] - cums_t, cums_t[:, 0:1] - cums_t)) * dt_t
        rows_sc[c, 0:2 * hp, :] = cums_t * LOG2E
        rows_sc[c, 2 * hp:4 * hp, :] = dt_t
        rows_sc[c, 4 * hp:6 * hp, :] = dte_t
        keepf_sc[c] = head_rows(jnp.exp(f[CHUNK - 1:CHUNK, :]), ef_ref)
        keepb_sc[c] = head_rows(jnp.exp(r[0:1, :]), eb_ref)
        ef_sc[c] = expand(jnp.exp(f), 0)
        er_sc[c] = expand(jnp.exp(r), hp)
        return carry

    lax.fori_loop(0, nchunks, conv, 0, unroll=4)

    sf_sc[...] = jnp.zeros_like(sf_sc)
    sb_sc[...] = jnp.zeros_like(sb_sc)

    def ascend(c, carry):
        r0 = pl.multiple_of(c * CHUNK, CHUNK)
        fr = fr_sc[c]
        cums_t = rows_sc[c, 0:2 * hp, :]
        dt_t = rows_sc[c, 2 * hp:4 * hp, :]
        dte_t = rows_sc[c, 4 * hp:6 * hp, :]
        keep_f = keepf_sc[c]
        xs = xs_sc[pl.ds(r0, CHUNK), :]
        cc = cs_sc[pl.ds(r0, CHUNK), :]
        cb = _dot_nt(cc, bs_sc[pl.ds(r0, CHUNK), :])
        yoff = _dot(cc, sf_sc[...].astype(BF16)) * ef_sc[c]
        bt = bt_sc[c]
        for pair in range(hp // 2):
            ms, bfs, bbs = [], [], []
            for i in (2 * pair, 2 * pair + 1):
                seg = jnp.where(lower, fr[:, i:i + 1] - cums_t[i:i + 1, :], fr[:, hp + i:hp + i + 1] - cums_t[hp + i:hp + i + 1, :])
                dtf_row = dt_t[i:i + 1, :]
                dtb_row = dt_t[hp + i:hp + i + 1, :]
                w = jnp.where(before, dtf_row, jnp.where(after, dtb_row, dtf_row + dtb_row))
                ms.append((cb * jnp.exp2(seg) * w).astype(BF16))
                bfs.append((bt * dte_t[i:i + 1, :]).astype(BF16))
                bbs.append((bt * dte_t[hp + i:hp + i + 1, :]).astype(BF16))
            sl = slice(pair * LANES, (pair + 1) * LANES)
            lhs = jnp.concatenate([jnp.concatenate(ms, axis=1), jnp.concatenate(bfs, axis=1),
                                   jnp.concatenate(bbs, axis=1)], axis=0)
            res = _dot(lhs, blockdiag(xs[:, sl]))
            yacc_sc[pl.ds(r0, CHUNK), sl] = res[0:CHUNK] + yoff[:, sl] + xs[:, sl] * dexp[:, sl]
            sf_sc[:, sl] = sf_sc[:, sl] * keep_f[0:1, sl] + res[CHUNK:2 * CHUNK]
            locb_sc[c, :, sl] = res[2 * CHUNK:3 * CHUNK]
        return carry

    lax.fori_loop(0, nchunks, ascend, 0, unroll=4)

    def descend(j, carry):
        c = nchunks - 1 - j
        r0 = pl.multiple_of(c * CHUNK, CHUNK)
        sb = sb_sc[...]
        y = yacc_sc[pl.ds(r0, CHUNK), :] + _dot(cs_sc[pl.ds(r0, CHUNK), :], sb.astype(BF16)) * er_sc[c]
        sb_sc[...] = sb * keepb_sc[c][0:1, :] + locb_sc[c]
        gated = y * _silu(z_ref[0, pl.ds(r0, CHUNK), :].astype(F32))
        ss = jnp.sum(gated * gated, axis=1, keepdims=True)
        ss_ref[0, pl.ds(r0, CHUNK), :] = jnp.broadcast_to(ss, (CHUNK, LANES))
        y_ref[0, pl.ds(r0, CHUNK), :] = (gated * ng_ref[...]).astype(BF16)
        return carry

    lax.fori_loop(0, nchunks, descend, 0, unroll=4)


def _ssd_constants():
    idx = np.arange(CHUNK)
    tril = (idx[None, :] <= idx[:, None]).astype(np.float32)
    triu = (idx[None, :] >= idx[:, None]).astype(np.float32)
    rows = np.arange(3 * LANES)[:, None] % LANES
    cols = np.arange(GROUP_WIDTH)[None, :] // SSD_HEAD_DIM
    ef = (rows == cols).astype(np.float32)
    eb = (rows == cols + HEADS_PER_GROUP).astype(np.float32)
    taps = [k for k in range(SSD_CONV) if k != SSD_CONV // 2]
    src = np.arange(CHUNK + 2 * CONV_PAD)[None, :]
    shift = np.concatenate([(src == idx[:, None] + CONV_PAD + k - SSD_CONV // 2) for k in taps], axis=0).astype(np.float32)
    return tuple(jnp.asarray(a, dtype=BF16) for a in (tril, triu, ef, eb, shift))


def _ssd(xbc, z, dt, conv_w, conv_b, gp, d_exp, norm_g):
    b, seq, _ = xbc.shape
    assert seq % CHUNK == 0
    gw = GROUP_WIDTH
    nx = SSD_WIDTH // LANES
    nchunks = seq // CHUNK
    consts = _ssd_constants()
    seq_blk = lambda w, f: pl.BlockSpec((1, seq, w), lambda bi, g: (bi, 0, f(g)))
    par = lambda rows, w, f: pl.BlockSpec((rows, w), lambda bi, g: (0, f(g)))
    full = lambda a: pl.BlockSpec(a.shape, lambda bi, g: (0,) * a.ndim)
    return pl.pallas_call(
        _ssd_kernel,
        grid=(b, SSD_GROUPS),
        in_specs=[seq_blk(gw, lambda g: g), seq_blk(LANES, lambda g: nx + g), seq_blk(LANES, lambda g: nx + SSD_GROUPS + g),
                  seq_blk(gw, lambda g: g), seq_blk(LANES, lambda g: g),
                  par(SSD_CONV, gw, lambda g: g), par(SSD_CONV, LANES, lambda g: nx + g),
                  par(SSD_CONV, LANES, lambda g: nx + SSD_GROUPS + g),
                  par(1, gw, lambda g: g), par(1, LANES, lambda g: nx + g), par(1, LANES, lambda g: nx + SSD_GROUPS + g),
                  pl.BlockSpec((1, 8, LANES), lambda bi, g: (g, 0, 0)), par(1, gw, lambda g: g), par(1, gw, lambda g: g)]
                 + [full(a) for a in consts],
        out_specs=[seq_blk(gw, lambda g: g), seq_blk(LANES, lambda g: g)],
        out_shape=[jax.ShapeDtypeStruct((b, seq, SSD_WIDTH), BF16), jax.ShapeDtypeStruct((b, seq, SSD_GROUPS * LANES), F32)],
        scratch_shapes=[pltpu.VMEM((seq + 2 * CONV_PAD, gw + 2 * LANES), BF16), pltpu.VMEM((seq, gw), F32),
                        pltpu.VMEM((seq, LANES), BF16), pltpu.VMEM((seq, LANES), BF16),
                        pltpu.VMEM((nchunks, SSD_STATE, CHUNK), F32), pltpu.VMEM((seq, gw), F32),
                        pltpu.VMEM((nchunks, SSD_STATE, gw), F32), pltpu.VMEM((nchunks, CHUNK, gw), F32),
                        pltpu.VMEM((nchunks, CHUNK, gw), F32), pltpu.VMEM((nchunks, 8, gw), F32), pltpu.VMEM((nchunks, 8, gw), F32),
                        pltpu.VMEM((nchunks, CHUNK, LANES), F32), pltpu.VMEM((nchunks, 6 * HEADS_PER_GROUP, LANES), F32),
                        pltpu.VMEM((SSD_STATE, gw), F32), pltpu.VMEM((SSD_STATE, gw), F32)],
        compiler_params=pltpu.CompilerParams(dimension_semantics=("arbitrary", "arbitrary"),
                                             vmem_limit_bytes=VMEM_LIMIT),
        name="bidir_ssd",
    )(xbc, xbc, xbc, z, dt, conv_w, conv_w, conv_w, conv_b, conv_b, conv_b, gp, d_exp, norm_g, *consts)


def _outproj_kernel(attn_ref, y_ref, ss_ref, h_ref, wa_ref, ws_ref, g_ref, b_ref, o_ref):
    slab = h_ref.shape[0] // FFN_OUT_SLABS
    for s in range(FFN_OUT_SLABS):
        rows = slice(s * slab, (s + 1) * slab)
        total = ss_ref[rows, 0:1]
        for g in range(1, SSD_GROUPS):
            total = total + ss_ref[rows, g * LANES:g * LANES + 1]
        rs = lax.rsqrt(total * (1.0 / SSD_WIDTH) + EPS)
        mix = _dot(attn_ref[rows, :], wa_ref[...]) + rs * _dot(y_ref[rows, :], ws_ref[...])
        o_ref[rows, :] = _layer_norm(DEEPNORM_ALPHA * h_ref[rows, :] + mix, g_ref[...], b_ref[...])


def _outproj(attn, y, ss, h, wa, ws, g, b):
    t = h.shape[0]
    tm = TOKEN_TILE
    row = lambda w: pl.BlockSpec((tm, w), lambda i: (i, 0))
    full = lambda a: pl.BlockSpec(a.shape, lambda i: (0,) * a.ndim)
    return pl.pallas_call(
        _outproj_kernel,
        grid=(t // tm,),
        in_specs=[row(ATTN_WIDTH), row(SSD_WIDTH), row(SSD_GROUPS * LANES), row(D_MODEL), full(wa), full(ws), full(g), full(b)],
        out_specs=row(D_MODEL),
        out_shape=jax.ShapeDtypeStruct((t, D_MODEL), F32),
        compiler_params=pltpu.CompilerParams(dimension_semantics=("arbitrary",), vmem_limit_bytes=VMEM_LIMIT),
        name="out_proj_ln",
    )(attn, y, ss, h, wa, ws, g, b)


def _ffn_kernel(tiles_per_seq, hp_ref, h_ref, hn_ref, wup_ref, cw_ref, cb_ref, wdn_ref, g_ref, b_ref, o_ref,
                hb_sc, act_sc):
    tm = h_ref.shape[0]
    pos = pl.program_id(0) % tiles_per_seq
    h = h_ref[...]
    hb_sc[0:FF_HALO, :] = jnp.where(pos == 0, 0.0, hp_ref[...]).astype(BF16)
    hb_sc[FF_HALO:FF_HALO + tm, :] = h.astype(BF16)
    hb_sc[FF_HALO + tm:, :] = jnp.where(pos == tiles_per_seq - 1, 0.0, hn_ref[...]).astype(BF16)
    for j in range(D_FF // FF_TILE):
        cols = slice(j * FF_TILE, (j + 1) * FF_TILE)
        vcols = slice(D_FF + j * FF_TILE, D_FF + (j + 1) * FF_TILE)
        g_ext = _dot(hb_sc[...], wup_ref[:, cols])
        val = _dot(hb_sc[FF_HALO:FF_HALO + tm, :], wup_ref[:, vcols])
        gate = cb_ref[:, cols] + cw_ref[1:2, cols] * g_ext[FF_HALO:FF_HALO + tm]
        gate = gate + cw_ref[0:1, cols] * g_ext[FF_HALO - 1:FF_HALO - 1 + tm]
        gate = gate + cw_ref[2:3, cols] * g_ext[FF_HALO + 1:FF_HALO + 1 + tm]
        act_sc[:, cols] = (_silu(gate) * val).astype(BF16)
    slab = tm // FFN_OUT_SLABS
    for s in range(FFN_OUT_SLABS):
        rows = slice(s * slab, (s + 1) * slab)
        ffn = _dot(act_sc[rows, :], wdn_ref[...])
        o_ref[rows, :] = _layer_norm(DEEPNORM_ALPHA * h_ref[rows, :] + ffn, g_ref[...], b_ref[...])


def _ffn(h, seq, wup, cw, cb, wdn, g, b):
    t = h.shape[0]
    tm = FFN_TOKEN_TILE
    assert seq % tm == 0 and tm % FF_HALO == 0 and D_FF % FF_TILE == 0
    tiles_per_seq = seq // tm
    hb = tm // FF_HALO
    nhb = t // FF_HALO
    row = pl.BlockSpec((tm, D_MODEL), lambda i: (i, 0))
    prev = pl.BlockSpec((FF_HALO, D_MODEL), lambda i: (jnp.maximum(i * hb - 1, 0), 0))
    nxt = pl.BlockSpec((FF_HALO, D_MODEL), lambda i: (jnp.minimum((i + 1) * hb, nhb - 1), 0))
    full = lambda a: pl.BlockSpec(a.shape, lambda i: (0,) * a.ndim, pipeline_mode=pl.Buffered(1))
    return pl.pallas_call(
        functools.partial(_ffn_kernel, tiles_per_seq),
        grid=(t // tm,),
        in_specs=[prev, row, nxt, full(wup), full(cw), full(cb), full(wdn), full(g), full(b)],
        out_specs=row,
        out_shape=jax.ShapeDtypeStruct((t, D_MODEL), F32),
        scratch_shapes=[pltpu.VMEM((tm + 2 * FF_HALO, D_MODEL), BF16), pltpu.VMEM((tm, D_FF), BF16)],
        compiler_params=pltpu.CompilerParams(dimension_semantics=("arbitrary",), vmem_limit_bytes=VMEM_LIMIT),
        name="ffn_ln",
    )(h, h, h, wup, cw, cb, wdn, g, b)


def _rope_tables(seq):
    half = HEAD_DIM // 2
    inv_freq = ROPE_THETA ** (-jnp.arange(0, HEAD_DIM, 2, dtype=F32) / HEAD_DIM)
    ang = jnp.arange(seq, dtype=F32)[:, None] * inv_freq[None, :]
    cos, sin = jnp.cos(ang), jnp.sin(ang)
    reps = LANES // HEAD_DIM
    return jnp.tile(jnp.concatenate([cos, cos], axis=1), (1, reps)), jnp.tile(jnp.concatenate([-sin, sin], axis=1), (1, reps))


def _per_group(fwd, bwd, offset_f, offset_b):
    out = jnp.zeros((SSD_GROUPS, LANES), F32)
    for g in range(SSD_GROUPS):
        sl = slice(g * HEADS_PER_GROUP, (g + 1) * HEADS_PER_GROUP)
        if fwd is not None:
            out = out.at[g, offset_f:offset_f + HEADS_PER_GROUP].set(fwd[sl])
        if bwd is not None:
            out = out.at[g, offset_b:offset_b + HEADS_PER_GROUP].set(bwd[sl])
    return out


def _layer_params(w_in, ssd_a_log, ssd_dt_bias, ssd_d, w_out, w_up, w_down):
    o_k = ATTN_WIDTH + KV_WIDTH
    o_z = o_k + KV_WIDTH
    o_x = o_z + SSD_WIDTH
    o_dt = o_x + XBC_WIDTH
    hp = HEADS_PER_GROUP
    wdt_src = w_in[:, o_dt:]
    wdt = jnp.zeros((D_MODEL, SSD_GROUPS * LANES), F32)
    for g in range(SSD_GROUPS):
        wdt = wdt.at[:, g * LANES:g * LANES + hp].set(wdt_src[:, g * hp:(g + 1) * hp])
        wdt = wdt.at[:, g * LANES + hp:g * LANES + 2 * hp].set(wdt_src[:, SSD_HEADS + g * hp:SSD_HEADS + (g + 1) * hp])
        wdt = wdt.at[:, g * LANES + 2 * hp:g * LANES + 4 * hp].set(wdt[:, g * LANES:g * LANES + 2 * hp])
    gp = jnp.zeros((SSD_GROUPS, 8, LANES), F32)
    dt_bias = _per_group(ssd_dt_bias[0], ssd_dt_bias[1], 0, hp)
    gp = gp.at[:, 0, :].set(dt_bias.at[:, 2 * hp:4 * hp].set(dt_bias[:, 0:2 * hp]))
    gp = gp.at[:, 1, :].set(_per_group(ssd_a_log[0], ssd_a_log[1], 0, hp))
    return dict(
        wqk=w_in[:, :o_k].astype(BF16), wv=w_in[:, o_k:o_z].astype(BF16), wz=w_in[:, o_z:o_x].astype(BF16),
        wx=w_in[:, o_x:o_dt].astype(BF16), wdt=wdt.astype(BF16), gp=gp,
        d_exp=jnp.repeat(ssd_d, SSD_HEAD_DIM)[None, :],
        wa=w_out[:ATTN_WIDTH].astype(BF16), ws=w_out[ATTN_WIDTH:].astype(BF16),
        wup=w_up.astype(BF16), wdn=w_down.astype(BF16))


def _trunk(x, ln_in_g, ln_in_b, layers):
    b, seq, _ = x.shape
    cos_t, sin_t = _rope_tables(seq)
    h = x.reshape(b * seq, D_MODEL)
    lng, lnb = ln_in_g[None, :], ln_in_b[None, :]
    for li, p in enumerate(layers):
        outs = _inproj(h, seq, li == 0, lng, lnb, p["wqk"], p["wv"], p["wz"], p["wx"], p["wdt"], cos_t, sin_t)
        if li == 0:
            h, outs = outs[0], outs[1:]
        q, kv, z, xbc, dt = outs
        three = lambda a: a.reshape(b, seq, a.shape[-1])
        attn = _attention(three(q), three(kv), p["sink"])
        y, ss = _ssd(three(xbc), three(z), three(dt), p["conv_w"], p["conv_b"], p["gp"], p["d_exp"], p["norm_g"])
        h = _outproj(attn.reshape(b * seq, -1), y.reshape(b * seq, -1), ss.reshape(b * seq, -1), h,
                     p["wa"], p["ws"], p["ln1_g"], p["ln1_b"])
        h = _ffn(h, seq, p["wup"], p["ffn_cw"], p["ffn_cb"], p["wdn"], p["ln2_g"], p["ln2_b"])
    return h.reshape(b, seq, D_MODEL)


def kernel(x_prompt, x_sample, ln_in_g, ln_in_b, w_in, attn_sink, ssd_conv_w, ssd_conv_b, ssd_a_log, ssd_dt_bias, ssd_d, ssd_norm_g, w_out, ln1_g, ln1_b, w_up, ffn_conv_w, ffn_conv_b, w_down, ln2_g, ln2_b):
    layers = []
    for i in range(w_in.shape[0]):
        p = _layer_params(w_in[i], ssd_a_log[i], ssd_dt_bias[i], ssd_d[i], w_out[i], w_up[i], w_down[i])
        p.update(sink=attn_sink[i], conv_w=ssd_conv_w[i], conv_b=ssd_conv_b[i][None, :], norm_g=ssd_norm_g[i][None, :],
                 ln1_g=ln1_g[i][None, :], ln1_b=ln1_b[i][None, :], ffn_cw=ffn_conv_w[i], ffn_cb=ffn_conv_b[i][None, :],
                 ln2_g=ln2_g[i][None, :], ln2_b=ln2_b[i][None, :])
        layers.append(p)
    return (_trunk(x_prompt, ln_in_g, ln_in_b, layers), _trunk(x_sample, ln_in_g, ln_in_b, layers))
```

```python
import functools
import math

import jax
import jax.numpy as jnp
import numpy as np
from jax import lax
from jax.experimental import pallas as pl
from jax.experimental.pallas import tpu as pltpu

F32 = jnp.float32
BF16 = jnp.bfloat16

D_MODEL = 1024
DEPTH = 4
HEAD_DIM = 64
N_Q_HEADS = 8
N_KV_HEADS = 2
ATTN_WIDTH = N_Q_HEADS * HEAD_DIM
KV_WIDTH = N_KV_HEADS * HEAD_DIM
WINDOW = 128
ROPE_THETA = 10000.0
SSD_HEADS = 16
SSD_HEAD_DIM = 64
SSD_WIDTH = SSD_HEADS * SSD_HEAD_DIM
SSD_GROUPS = 2
HEADS_PER_GROUP = SSD_HEADS // SSD_GROUPS
GROUP_WIDTH = HEADS_PER_GROUP * SSD_HEAD_DIM
SSD_STATE = 128
SSD_CONV = 5
CHUNK = 128
XBC_WIDTH = SSD_WIDTH + 2 * SSD_GROUPS * SSD_STATE
D_FF = 2816
FFN_CONV = 3
DEEPNORM_ALPHA = (2 * DEPTH) ** 0.25
EPS = 1e-5
MASK_VALUE = -1e30
LOG2E = math.log2(math.e)

LANES = 128
VMEM_LIMIT = 56 * 1024 * 1024
TOKEN_TILE = 1024
FFN_TOKEN_TILE = 1024
ATTN_BQ = 1024
FF_TILE = 256
FFN_OUT_SLABS = 4
FF_HALO = 16
CONV_PAD = 16


def _dot(a, b):
    return jnp.dot(a, b, preferred_element_type=F32)


def _dot_nt(a, b):
    return lax.dot_general(a, b, (((1,), (1,)), ((), ())), preferred_element_type=F32)


def _layer_norm(x, g, b):
    mu = jnp.mean(x, axis=-1, keepdims=True)
    xc = x - mu
    var = jnp.mean(xc * xc, axis=-1, keepdims=True)
    return xc * lax.rsqrt(var + EPS) * g + b


def _silu(x):
    return x * (1.0 / (1.0 + jnp.exp(-x)))


def _split3(x):
    hi = x.astype(BF16)
    r1 = x - hi.astype(F32)
    mid = r1.astype(BF16)
    lo = (r1 - mid.astype(F32)).astype(BF16)
    return jnp.concatenate([hi, mid, lo], axis=1)


def _inproj_kernel(apply_ln, x_ref, lng_ref, lnb_ref, wqk_ref, wv_ref, wz_ref, wx_ref, wdt_ref,
                   cos_ref, sin_ref, *outs):
    if apply_ln:
        h_ref, q_ref, kv_ref, z_ref, xbc_ref, dt_ref = outs
    else:
        q_ref, kv_ref, z_ref, xbc_ref, dt_ref = outs
    x = x_ref[...]
    if apply_ln:
        x = _layer_norm(x, lng_ref[...], lnb_ref[...])
        h_ref[...] = x
    xb = x.astype(BF16)
    tm = x.shape[0]
    cos = cos_ref[...]
    sin = sin_ref[...]
    lane = lax.broadcasted_iota(jnp.int32, (tm, LANES), 1)
    first_half = (lane % HEAD_DIM) < (HEAD_DIM // 2)
    low_head = lane < HEAD_DIM

    def rope(t):
        swapped = jnp.where(first_half, pltpu.roll(t, LANES - HEAD_DIM // 2, 1), pltpu.roll(t, HEAD_DIM // 2, 1))
        return t * cos + swapped * sin

    def both_halves(t):
        sw = pltpu.roll(t, HEAD_DIM, 1)
        return jnp.where(low_head, t, sw), jnp.where(low_head, sw, t)

    qk = _dot(xb, wqk_ref[...])
    scale = HEAD_DIM ** -0.5 * LOG2E
    for j in range(ATTN_WIDTH // LANES):
        q_ref[:, j * LANES:(j + 1) * LANES] = (rope(qk[:, j * LANES:(j + 1) * LANES]) * scale).astype(BF16)
    k0, k1 = both_halves(rope(qk[:, ATTN_WIDTH:ATTN_WIDTH + KV_WIDTH]))
    v0, v1 = both_halves(_dot(xb, wv_ref[...]))
    kv_ref[:, 0 * LANES:1 * LANES] = k0.astype(BF16)
    kv_ref[:, 1 * LANES:2 * LANES] = k1.astype(BF16)
    kv_ref[:, 2 * LANES:3 * LANES] = v0.astype(BF16)
    kv_ref[:, 3 * LANES:4 * LANES] = v1.astype(BF16)
    z_ref[...] = _dot(xb, wz_ref[...]).astype(BF16)
    xbc_ref[...] = _dot(xb, wx_ref[...]).astype(BF16)
    dt_ref[...] = _dot(xb, wdt_ref[...])


def _inproj(x2d, seq, apply_ln, lng, lnb, wqk, wv, wz, wx, wdt, cos_t, sin_t):
    t = x2d.shape[0]
    tm = TOKEN_TILE
    assert t % tm == 0 and seq % tm == 0
    tiles_per_seq = seq // tm
    row = lambda w: pl.BlockSpec((tm, w), lambda i: (i, 0))
    full = lambda a: pl.BlockSpec(a.shape, lambda i: (0,) * a.ndim)
    tab = pl.BlockSpec((tm, LANES), lambda i: (i % tiles_per_seq, 0))
    out_shape = [jax.ShapeDtypeStruct((t, ATTN_WIDTH), BF16), jax.ShapeDtypeStruct((t, 4 * LANES), BF16),
                 jax.ShapeDtypeStruct((t, SSD_WIDTH), BF16), jax.ShapeDtypeStruct((t, XBC_WIDTH), BF16),
                 jax.ShapeDtypeStruct((t, SSD_GROUPS * LANES), F32)]
    out_specs = [row(ATTN_WIDTH), row(4 * LANES), row(SSD_WIDTH), row(XBC_WIDTH), row(SSD_GROUPS * LANES)]
    if apply_ln:
        out_shape = [jax.ShapeDtypeStruct((t, D_MODEL), F32)] + out_shape
        out_specs = [row(D_MODEL)] + out_specs
    return pl.pallas_call(
        functools.partial(_inproj_kernel, apply_ln),
        grid=(t // tm,),
        in_specs=[row(D_MODEL), full(lng), full(lnb), full(wqk), full(wv), full(wz), full(wx), full(wdt), tab, tab],
        out_specs=out_specs,
        out_shape=out_shape,
        compiler_params=pltpu.CompilerParams(dimension_semantics=("arbitrary",), vmem_limit_bytes=VMEM_LIMIT),
        name="in_proj_ln" if apply_ln else "in_proj",
    )(x2d, lng, lnb, wqk, wv, wz, wx, wdt, cos_t, sin_t)


def _attn_kernel(seq, sink_ref, q_ref, kl_ref, km_ref, kr_ref, vl_ref, vm_ref, vr_ref, o_ref):
    bq = q_ref.shape[1]
    sub = WINDOW
    ks = sub + 2 * WINDOW
    start = pl.program_id(1) * bq
    qi = lax.broadcasted_iota(jnp.int32, (sub, ks), 0)
    kc = lax.broadcasted_iota(jnp.int32, (sub, ks), 1) - WINDOW
    band = jnp.abs(kc - qi) <= WINDOW
    lane = lax.broadcasted_iota(jnp.int32, (sub, LANES), 1)
    low_head = lane < HEAD_DIM
    group = N_Q_HEADS // N_KV_HEADS
    for h in range(N_KV_HEADS):
        ksl = slice(h * LANES, (h + 1) * LANES)
        kk = jnp.concatenate([kl_ref[0, :, ksl], km_ref[0, :, ksl], kr_ref[0, :, ksl]], axis=0)
        vv = jnp.concatenate([vl_ref[0, :, ksl], vm_ref[0, :, ksl], vr_ref[0, :, ksl]], axis=0)
        for sb in range(bq // sub):
            rows = slice(sb * sub, (sb + 1) * sub)
            kpos = kc + (start + sb * sub)
            allow = band & (kpos >= 0) & (kpos < seq)
            qs = []
            for r in range(group):
                head = group * h + r
                qt = q_ref[0, rows, (head // 2) * LANES:(head // 2 + 1) * LANES]
                keep = low_head if head % 2 == 0 else jnp.logical_not(low_head)
                qs.append(jnp.where(keep, qt, jnp.zeros_like(qt)))
            sc = _dot_nt(jnp.concatenate(qs, axis=0), kk[sb * sub:sb * sub + ks])
            ps, inv = [], []
            for r in range(group):
                sink = sink_ref[group * h + r] * LOG2E
                s = sc[r * sub:(r + 1) * sub]
                s = jnp.concatenate([jnp.where(allow[:, :WINDOW], s[:, :WINDOW], MASK_VALUE), s[:, WINDOW:ks - WINDOW],
                                     jnp.where(allow[:, ks - WINDOW:], s[:, ks - WINDOW:], MASK_VALUE)], axis=1)
                m = jnp.maximum(jnp.max(s, axis=-1, keepdims=True), sink)
                p = jnp.exp2(s - m)
                denom = jnp.sum(p, axis=-1, keepdims=True) + jnp.exp2(sink - m)
                ps.append(p.astype(BF16))
                inv.append(1.0 / denom)
            pv = _dot(jnp.concatenate(ps, axis=0), vv[sb * sub:sb * sub + ks])
            for pair in range(group // 2):
                even = pv[(2 * pair) * sub:(2 * pair + 1) * sub] * inv[2 * pair]
                odd = pv[(2 * pair + 1) * sub:(2 * pair + 2) * sub] * inv[2 * pair + 1]
                col = (group * h) // 2 + pair
                o_ref[0, rows, col * LANES:(col + 1) * LANES] = jnp.where(low_head, even, odd).astype(BF16)


def _attention(q, kv, sink):
    b, seq, _ = q.shape
    bq = ATTN_BQ
    assert seq % bq == 0 and bq % WINDOW == 0
    r = bq // WINDOW
    nwb = seq // WINDOW
    side = lambda lane_blk, f: pl.BlockSpec((1, WINDOW, 2 * LANES), lambda bi, i: (bi, f(i), lane_blk))
    mid = lambda lane_blk: pl.BlockSpec((1, bq, 2 * LANES), lambda bi, i: (bi, i, lane_blk))
    left = lambda i: jnp.maximum(i * r - 1, 0)
    right = lambda i: jnp.minimum((i + 1) * r, nwb - 1)
    return pl.pallas_call(
        functools.partial(_attn_kernel, seq),
        grid=(b, seq // bq),
        in_specs=[pl.BlockSpec(memory_space=pltpu.SMEM),
                  pl.BlockSpec((1, bq, ATTN_WIDTH), lambda bi, i: (bi, i, 0)),
                  side(0, left), mid(0), side(0, right), side(1, left), mid(1), side(1, right)],
        out_specs=pl.BlockSpec((1, bq, ATTN_WIDTH), lambda bi, i: (bi, i, 0)),
        out_shape=jax.ShapeDtypeStruct((b, seq, ATTN_WIDTH), BF16),
        compiler_params=pltpu.CompilerParams(dimension_semantics=("arbitrary", "arbitrary"),
                                             vmem_limit_bytes=VMEM_LIMIT),
        name="window_attn",
    )(sink, q, kv, kv, kv, kv, kv, kv)


def _ssd_kernel(x_ref, bm_ref, cm_ref, z_ref, dt_ref, cwx_ref, cwb_ref, cwc_ref, cbx_ref, cbb_ref, cbc_ref,
                gp_ref, dexp_ref, ng_ref, tril_ref, triu_ref, ef_ref, eb_ref, sh_ref, y_ref, ss_ref,
                pad_sc, xs_sc, bs_sc, cs_sc, bt_sc, yacc_sc, locb_sc, ef_sc, er_sc, keepf_sc, keepb_sc, fr_sc, rows_sc,
                sf_sc, sb_sc):
    seq = x_ref.shape[1]
    nchunks = seq // CHUNK
    gw = GROUP_WIDTH
    hp = HEADS_PER_GROUP
    pad = CONV_PAD
    taps = [k for k in range(SSD_CONV) if k != SSD_CONV // 2]

    zeros = jnp.zeros((pad, gw + 2 * LANES), BF16)
    pad_sc[0:pad, :] = zeros
    pad_sc[pad + seq:pad + seq + pad, :] = zeros

    def fill(c, carry):
        r0 = pl.multiple_of(c * CHUNK, CHUNK)
        pad_sc[pl.ds(pad + r0, CHUNK), 0:gw] = x_ref[0, pl.ds(r0, CHUNK), :]
        pad_sc[pl.ds(pad + r0, CHUNK), gw:gw + LANES] = bm_ref[0, pl.ds(r0, CHUNK), :]
        pad_sc[pl.ds(pad + r0, CHUNK), gw + LANES:gw + 2 * LANES] = cm_ref[0, pl.ds(r0, CHUNK), :]
        return carry

    lax.fori_loop(0, nchunks, fill, 0)

    lane_row = lax.broadcasted_iota(jnp.int32, (1, LANES), 1)
    dt_bias = gp_ref[0, 0:1, :]
    a_row = jnp.where(lane_row < 2 * hp, -jnp.exp(gp_ref[0, 1:2, :]), 0.0)
    dexp = dexp_ref[...]
    tril = tril_ref[...]
    triu = triu_ref[...]
    qi = lax.broadcasted_iota(jnp.int32, (CHUNK, CHUNK), 0)
    si = lax.broadcasted_iota(jnp.int32, (CHUNK, CHUNK), 1)
    lower = si <= qi
    before = si < qi
    after = si > qi
    low_head = si < SSD_HEAD_DIM
    fwd_lane = si < hp

    def blockdiag(t):
        return jnp.concatenate([jnp.where(low_head, t, 0.0), jnp.where(low_head, 0.0, t)], axis=0).astype(BF16)

    def head_rows(v, e_ref):
        return _dot(_split3(jnp.broadcast_to(v, (8, LANES))), e_ref[...])

    def chunk_scalars(r0):
        x = dt_ref[0, pl.ds(r0, CHUNK), :] + dt_bias
        dt = jnp.maximum(x, 0.0) + jnp.log1p(jnp.exp(-jnp.abs(x)))
        pieces = _split3(dt * a_row)
        fp = _dot(tril, pieces)
        rp = _dot(triu, pieces)
        f = fp[:, 0:LANES] + fp[:, LANES:2 * LANES] + fp[:, 2 * LANES:3 * LANES]
        r = rp[:, 0:LANES] + rp[:, LANES:2 * LANES] + rp[:, 2 * LANES:3 * LANES]
        return dt, f, r

    def expand(v, first):
        tiles = [jnp.where(low_head, v[:, first + 2 * p:first + 2 * p + 1], v[:, first + 2 * p + 1:first + 2 * p + 2])
                 for p in range(hp // 2)]
        return jnp.concatenate(tiles, axis=1)

    def conv(c, carry):
        r0 = pl.multiple_of(c * CHUNK, CHUNK)
        shifted = _dot(sh_ref[...], pad_sc[pl.ds(r0, CHUNK + 2 * pad), :])
        centre = pad_sc[pl.ds(r0 + pad, CHUNK), :].astype(F32)

        def one(lo, hi, w_ref, b_ref):
            acc = b_ref[...] + w_ref[SSD_CONV // 2:SSD_CONV // 2 + 1, :] * centre[:, lo:hi]
            for j, k in enumerate(taps):
                acc = acc + w_ref[k:k + 1, :] * shifted[j * CHUNK:(j + 1) * CHUNK, lo:hi]
            return _silu(acc)

        xs_sc[pl.ds(r0, CHUNK), :] = one(0, gw, cwx_ref, cbx_ref)
        bconv = one(gw, gw + LANES, cwb_ref, cbb_ref)
        bs_sc[pl.ds(r0, CHUNK), :] = bconv.astype(BF16)
        bt_sc[c] = bconv.T
        cs_sc[pl.ds(r0, CHUNK), :] = one(gw + LANES, gw + 2 * LANES, cwc_ref, cbc_ref).astype(BF16)

        dt, f, r = chunk_scalars(r0)
        fr = jnp.where(fwd_lane, f, r)
        fr_sc[c] = fr * LOG2E
        pt = jnp.where(si < 2 * hp, fr, dt).T
        cums_t = pt[0:2 * hp, :]
        dt_t = pt[2 * hp:4 * hp, :]
        dte_t = jnp.exp(jnp.where(qi[0:2 * hp] < hp, cums_t[:, CHUNK - 1:CHUNK] - cums_t, cums_t[:, 0:1] - cums_t)) * dt_t
        rows_sc[c, 0:2 * hp, :] = cums_t * LOG2E
        rows_sc[c, 2 * hp:4 * hp, :] = dt_t
        rows_sc[c, 4 * hp:6 * hp, :] = dte_t
        keepf_sc[c] = head_rows(jnp.exp(f[CHUNK - 1:CHUNK, :]), ef_ref)
        keepb_sc[c] = head_rows(jnp.exp(r[0:1, :]), eb_ref)
        ef_sc[c] = _dot(_split3(jnp.exp(f)), ef_ref[...])
        er_sc[c] = expand(jnp.exp(r), hp)
        return carry

    lax.fori_loop(0, nchunks, conv, 0, unroll=8)

    sf_sc[...] = jnp.zeros_like(sf_sc)
    sb_sc[...] = jnp.zeros_like(sb_sc)

    def ascend(c, carry):
        r0 = pl.multiple_of(c * CHUNK, CHUNK)
        fr = fr_sc[c]
        cums_t = rows_sc[c, 0:2 * hp, :]
        dt_t = rows_sc[c, 2 * hp:4 * hp, :]
        dte_t = rows_sc[c, 4 * hp:6 * hp, :]
        keep_f = keepf_sc[c]
        xs = xs_sc[pl.ds(r0, CHUNK), :]
        cc = cs_sc[pl.ds(r0, CHUNK), :]
        cb = _dot_nt(cc, bs_sc[pl.ds(r0, CHUNK), :])
        yoff = _dot(cc, sf_sc[...].astype(BF16)) * ef_sc[c]
        bt = bt_sc[c]
        for pair in range(hp // 2):
            ms, bfs, bbs = [], [], []
            for i in (2 * pair, 2 * pair + 1):
                seg = jnp.where(lower, fr[:, i:i + 1] - cums_t[i:i + 1, :], fr[:, hp + i:hp + i + 1] - cums_t[hp + i:hp + i + 1, :])
                dtf_row = dt_t[i:i + 1, :]
                dtb_row = dt_t[hp + i:hp + i + 1, :]
                w = jnp.where(before, dtf_row, jnp.where(after, dtb_row, dtf_row + dtb_row))
                ms.append((cb * jnp.exp2(seg) * w).astype(BF16))
                bfs.append((bt * dte_t[i:i + 1, :]).astype(BF16))
                bbs.append((bt * dte_t[hp + i:hp + i + 1, :]).astype(BF16))
            sl = slice(pair * LANES, (pair + 1) * LANES)
            lhs = jnp.concatenate([jnp.concatenate(ms, axis=1), jnp.concatenate(bfs, axis=1),
                                   jnp.concatenate(bbs, axis=1)], axis=0)
            res = _dot(lhs, blockdiag(xs[:, sl]))
            yacc_sc[pl.ds(r0, CHUNK), sl] = res[0:CHUNK] + yoff[:, sl] + xs[:, sl] * dexp[:, sl]
            sf_sc[:, sl] = sf_sc[:, sl] * keep_f[0:1, sl] + res[CHUNK:2 * CHUNK]
            locb_sc[c, :, sl] = res[2 * CHUNK:3 * CHUNK]
        return carry

    lax.fori_loop(0, nchunks, ascend, 0, unroll=8)

    def descend(j, carry):
        c = nchunks - 1 - j
        r0 = pl.multiple_of(c * CHUNK, CHUNK)
        sb = sb_sc[...]
        y = yacc_sc[pl.ds(r0, CHUNK), :] + _dot(cs_sc[pl.ds(r0, CHUNK), :], sb.astype(BF16)) * er_sc[c]
        sb_sc[...] = sb * keepb_sc[c][0:1, :] + locb_sc[c]
        gated = y * _silu(z_ref[0, pl.ds(r0, CHUNK), :].astype(F32))
        ss = jnp.sum(gated * gated, axis=1, keepdims=True)
        ss_ref[0, pl.ds(r0, CHUNK), :] = jnp.broadcast_to(ss, (CHUNK, LANES))
        y_ref[0, pl.ds(r0, CHUNK), :] = (gated * ng_ref[...]).astype(BF16)
        return carry

    lax.fori_loop(0, nchunks, descend, 0, unroll=8)


def _ssd_constants():
    idx = np.arange(CHUNK)
    tril = (idx[None, :] <= idx[:, None]).astype(np.float32)
    triu = (idx[None, :] >= idx[:, None]).astype(np.float32)
    rows = np.arange(3 * LANES)[:, None] % LANES
    cols = np.arange(GROUP_WIDTH)[None, :] // SSD_HEAD_DIM
    ef = (rows == cols).astype(np.float32)
    eb = (rows == cols + HEADS_PER_GROUP).astype(np.float32)
    taps = [k for k in range(SSD_CONV) if k != SSD_CONV // 2]
    src = np.arange(CHUNK + 2 * CONV_PAD)[None, :]
    shift = np.concatenate([(src == idx[:, None] + CONV_PAD + k - SSD_CONV // 2) for k in taps], axis=0).astype(np.float32)
    return tuple(jnp.asarray(a, dtype=BF16) for a in (tril, triu, ef, eb, shift))


def _ssd(xbc, z, dt, conv_w, conv_b, gp, d_exp, norm_g):
    b, seq, _ = xbc.shape
    assert seq % CHUNK == 0
    gw = GROUP_WIDTH
    nx = SSD_WIDTH // LANES
    nchunks = seq // CHUNK
    consts = _ssd_constants()
    seq_blk = lambda w, f: pl.BlockSpec((1, seq, w), lambda bi, g: (bi, 0, f(g)))
    par = lambda rows, w, f: pl.BlockSpec((rows, w), lambda bi, g: (0, f(g)))
    full = lambda a: pl.BlockSpec(a.shape, lambda bi, g: (0,) * a.ndim)
    return pl.pallas_call(
        _ssd_kernel,
        grid=(b, SSD_GROUPS),
        in_specs=[seq_blk(gw, lambda g: g), seq_blk(LANES, lambda g: nx + g), seq_blk(LANES, lambda g: nx + SSD_GROUPS + g),
                  seq_blk(gw, lambda g: g), seq_blk(LANES, lambda g: g),
                  par(SSD_CONV, gw, lambda g: g), par(SSD_CONV, LANES, lambda g: nx + g),
                  par(SSD_CONV, LANES, lambda g: nx + SSD_GROUPS + g),
                  par(1, gw, lambda g: g), par(1, LANES, lambda g: nx + g), par(1, LANES, lambda g: nx + SSD_GROUPS + g),
                  pl.BlockSpec((1, 8, LANES), lambda bi, g: (g, 0, 0)), par(1, gw, lambda g: g), par(1, gw, lambda g: g)]
                 + [full(a) for a in consts],
        out_specs=[seq_blk(gw, lambda g: g), seq_blk(LANES, lambda g: g)],
        out_shape=[jax.ShapeDtypeStruct((b, seq, SSD_WIDTH), BF16), jax.ShapeDtypeStruct((b, seq, SSD_GROUPS * LANES), F32)],
        scratch_shapes=[pltpu.VMEM((seq + 2 * CONV_PAD, gw + 2 * LANES), BF16), pltpu.VMEM((seq, gw), F32),
                        pltpu.VMEM((seq, LANES), BF16), pltpu.VMEM((seq, LANES), BF16),
                        pltpu.VMEM((nchunks, SSD_STATE, CHUNK), F32), pltpu.VMEM((seq, gw), F32),
                        pltpu.VMEM((nchunks, SSD_STATE, gw), F32), pltpu.VMEM((nchunks, CHUNK, gw), F32),
                        pltpu.VMEM((nchunks, CHUNK, gw), F32), pltpu.VMEM((nchunks, 8, gw), F32), pltpu.VMEM((nchunks, 8, gw), F32),
                        pltpu.VMEM((nchunks, CHUNK, LANES), F32), pltpu.VMEM((nchunks, 6 * HEADS_PER_GROUP, LANES), F32),
                        pltpu.VMEM((SSD_STATE, gw), F32), pltpu.VMEM((SSD_STATE, gw), F32)],
        compiler_params=pltpu.CompilerParams(dimension_semantics=("arbitrary", "arbitrary"),
                                             vmem_limit_bytes=VMEM_LIMIT),
        name="bidir_ssd",
    )(xbc, xbc, xbc, z, dt, conv_w, conv_w, conv_w, conv_b, conv_b, conv_b, gp, d_exp, norm_g, *consts)


def _outproj_kernel(attn_ref, y_ref, ss_ref, h_ref, wa_ref, ws_ref, g_ref, b_ref, o_ref):
    slab = h_ref.shape[0] // FFN_OUT_SLABS
    for s in range(FFN_OUT_SLABS):
        rows = slice(s * slab, (s + 1) * slab)
        total = ss_ref[rows, 0:1]
        for g in range(1, SSD_GROUPS):
            total = total + ss_ref[rows, g * LANES:g * LANES + 1]
        rs = lax.rsqrt(total * (1.0 / SSD_WIDTH) + EPS)
        mix = _dot(attn_ref[rows, :], wa_ref[...]) + rs * _dot(y_ref[rows, :], ws_ref[...])
        o_ref[rows, :] = _layer_norm(DEEPNORM_ALPHA * h_ref[rows, :] + mix, g_ref[...], b_ref[...])


def _outproj(attn, y, ss, h, wa, ws, g, b):
    t = h.shape[0]
    tm = TOKEN_TILE
    row = lambda w: pl.BlockSpec((tm, w), lambda i: (i, 0))
    full = lambda a: pl.BlockSpec(a.shape, lambda i: (0,) * a.ndim)
    return pl.pallas_call(
        _outproj_kernel,
        grid=(t // tm,),
        in_specs=[row(ATTN_WIDTH), row(SSD_WIDTH), row(SSD_GROUPS * LANES), row(D_MODEL), full(wa), full(ws), full(g), full(b)],
        out_specs=row(D_MODEL),
        out_shape=jax.ShapeDtypeStruct((t, D_MODEL), F32),
        compiler_params=pltpu.CompilerParams(dimension_semantics=("arbitrary",), vmem_limit_bytes=VMEM_LIMIT),
        name="out_proj_ln",
    )(attn, y, ss, h, wa, ws, g, b)


def _ffn_kernel(tiles_per_seq, hp_ref, h_ref, hn_ref, wup_ref, cw_ref, cb_ref, wdn_ref, g_ref, b_ref, o_ref,
                hb_sc, act_sc):
    tm = h_ref.shape[0]
    pos = pl.program_id(0) % tiles_per_seq
    h = h_ref[...]
    hb_sc[0:FF_HALO, :] = jnp.where(pos == 0, 0.0, hp_ref[...]).astype(BF16)
    hb_sc[FF_HALO:FF_HALO + tm, :] = h.astype(BF16)
    hb_sc[FF_HALO + tm:, :] = jnp.where(pos == tiles_per_seq - 1, 0.0, hn_ref[...]).astype(BF16)
    for j in range(D_FF // FF_TILE):
        cols = slice(j * FF_TILE, (j + 1) * FF_TILE)
        vcols = slice(D_FF + j * FF_TILE, D_FF + (j + 1) * FF_TILE)
        g_ext = _dot(hb_sc[...], wup_ref[:, cols])
        val = _dot(hb_sc[FF_HALO:FF_HALO + tm, :], wup_ref[:, vcols])
        gate = cb_ref[:, cols] + cw_ref[1:2, cols] * g_ext[FF_HALO:FF_HALO + tm]
        gate = gate + cw_ref[0:1, cols] * g_ext[FF_HALO - 1:FF_HALO - 1 + tm]
        gate = gate + cw_ref[2:3, cols] * g_ext[FF_HALO + 1:FF_HALO + 1 + tm]
        act_sc[:, cols] = (_silu(gate) * val).astype(BF16)
    slab = tm // FFN_OUT_SLABS
    for s in range(FFN_OUT_SLABS):
        rows = slice(s * slab, (s + 1) * slab)
        ffn = _dot(act_sc[rows, :], wdn_ref[...])
        o_ref[rows, :] = _layer_norm(DEEPNORM_ALPHA * h_ref[rows, :] + ffn, g_ref[...], b_ref[...])


def _ffn(h, seq, wup, cw, cb, wdn, g, b):
    t = h.shape[0]
    tm = FFN_TOKEN_TILE
    assert seq % tm == 0 and tm % FF_HALO == 0 and D_FF % FF_TILE == 0
    tiles_per_seq = seq // tm
    hb = tm // FF_HALO
    nhb = t // FF_HALO
    row = pl.BlockSpec((tm, D_MODEL), lambda i: (i, 0))
    prev = pl.BlockSpec((FF_HALO, D_MODEL), lambda i: (jnp.maximum(i * hb - 1, 0), 0))
    nxt = pl.BlockSpec((FF_HALO, D_MODEL), lambda i: (jnp.minimum((i + 1) * hb, nhb - 1), 0))
    full = lambda a: pl.BlockSpec(a.shape, lambda i: (0,) * a.ndim, pipeline_mode=pl.Buffered(1))
    return pl.pallas_call(
        functools.partial(_ffn_kernel, tiles_per_seq),
        grid=(t // tm,),
        in_specs=[prev, row, nxt, full(wup), full(cw), full(cb), full(wdn), full(g), full(b)],
        out_specs=row,
        out_shape=jax.ShapeDtypeStruct((t, D_MODEL), F32),
        scratch_shapes=[pltpu.VMEM((tm + 2 * FF_HALO, D_MODEL), BF16), pltpu.VMEM((tm, D_FF), BF16)],
        compiler_params=pltpu.CompilerParams(dimension_semantics=("arbitrary",), vmem_limit_bytes=VMEM_LIMIT),
        name="ffn_ln",
    )(h, h, h, wup, cw, cb, wdn, g, b)


def _rope_tables(seq):
    half = HEAD_DIM // 2
    inv_freq = ROPE_THETA ** (-jnp.arange(0, HEAD_DIM, 2, dtype=F32) / HEAD_DIM)
    ang = jnp.arange(seq, dtype=F32)[:, None] * inv_freq[None, :]
    cos, sin = jnp.cos(ang), jnp.sin(ang)
    reps = LANES // HEAD_DIM
    return jnp.tile(jnp.concatenate([cos, cos], axis=1), (1, reps)), jnp.tile(jnp.concatenate([-sin, sin], axis=1), (1, reps))


def _per_group(fwd, bwd, offset_f, offset_b):
    out = jnp.zeros((SSD_GROUPS, LANES), F32)
    for g in range(SSD_GROUPS):
        sl = slice(g * HEADS_PER_GROUP, (g + 1) * HEADS_PER_GROUP)
        if fwd is not None:
            out = out.at[g, offset_f:offset_f + HEADS_PER_GROUP].set(fwd[sl])
        if bwd is not None:
            out = out.at[g, offset_b:offset_b + HEADS_PER_GROUP].set(bwd[sl])
    return out


def _layer_params(w_in, ssd_a_log, ssd_dt_bias, ssd_d, w_out, w_up, w_down):
    o_k = ATTN_WIDTH + KV_WIDTH
    o_z = o_k + KV_WIDTH
    o_x = o_z + SSD_WIDTH
    o_dt = o_x + XBC_WIDTH
    hp = HEADS_PER_GROUP
    wdt_src = w_in[:, o_dt:]
    wdt = jnp.zeros((D_MODEL, SSD_GROUPS * LANES), F32)
    for g in range(SSD_GROUPS):
        wdt = wdt.at[:, g * LANES:g * LANES + hp].set(wdt_src[:, g * hp:(g + 1) * hp])
        wdt = wdt.at[:, g * LANES + hp:g * LANES + 2 * hp].set(wdt_src[:, SSD_HEADS + g * hp:SSD_HEADS + (g + 1) * hp])
        wdt = wdt.at[:, g * LANES + 2 * hp:g * LANES + 4 * hp].set(wdt[:, g * LANES:g * LANES + 2 * hp])
    gp = jnp.zeros((SSD_GROUPS, 8, LANES), F32)
    dt_bias = _per_group(ssd_dt_bias[0], ssd_dt_bias[1], 0, hp)
    gp = gp.at[:, 0, :].set(dt_bias.at[:, 2 * hp:4 * hp].set(dt_bias[:, 0:2 * hp]))
    gp = gp.at[:, 1, :].set(_per_group(ssd_a_log[0], ssd_a_log[1], 0, hp))
    return dict(
        wqk=w_in[:, :o_k].astype(BF16), wv=w_in[:, o_k:o_z].astype(BF16), wz=w_in[:, o_z:o_x].astype(BF16),
        wx=w_in[:, o_x:o_dt].astype(BF16), wdt=wdt.astype(BF16), gp=gp,
        d_exp=jnp.repeat(ssd_d, SSD_HEAD_DIM)[None, :],
        wa=w_out[:ATTN_WIDTH].astype(BF16), ws=w_out[ATTN_WIDTH:].astype(BF16),
        wup=w_up.astype(BF16), wdn=w_down.astype(BF16))


def _trunk(x, ln_in_g, ln_in_b, layers):
    b, seq, _ = x.shape
    cos_t, sin_t = _rope_tables(seq)
    h = x.reshape(b * seq, D_MODEL)
    lng, lnb = ln_in_g[None, :], ln_in_b[None, :]
    for li, p in enumerate(layers):
        outs = _inproj(h, seq, li == 0, lng, lnb, p["wqk"], p["wv"], p["wz"], p["wx"], p["wdt"], cos_t, sin_t)
        if li == 0:
            h, outs = outs[0], outs[1:]
        q, kv, z, xbc, dt = outs
        three = lambda a: a.reshape(b, seq, a.shape[-1])
        attn = _attention(three(q), three(kv), p["sink"])
        y, ss = _ssd(three(xbc), three(z), three(dt), p["conv_w"], p["conv_b"], p["gp"], p["d_exp"], p["norm_g"])
        h = _outproj(attn.reshape(b * seq, -1), y.reshape(b * seq, -1), ss.reshape(b * seq, -1), h,
                     p["wa"], p["ws"], p["ln1_g"], p["ln1_b"])
        h = _ffn(h, seq, p["wup"], p["ffn_cw"], p["ffn_cb"], p["wdn"], p["ln2_g"], p["ln2_b"])
    return h.reshape(b, seq, D_MODEL)


def kernel(x_prompt, x_sample, ln_in_g, ln_in_b, w_in, attn_sink, ssd_conv_w, ssd_conv_b, ssd_a_log, ssd_dt_bias, ssd_d, ssd_norm_g, w_out, ln1_g, ln1_b, w_up, ffn_conv_w, ffn_conv_b, w_down, ln2_g, ln2_b):
    layers = []
    for i in range(w_in.shape[0]):
        p = _layer_params(w_in[i], ssd_a_log[i], ssd_dt_bias[i], ssd_d[i], w_out[i], w_up[i], w_down[i])
        p.update(sink=attn_sink[i], conv_w=ssd_conv_w[i], conv_b=ssd_conv_b[i][None, :], norm_g=ssd_norm_g[i][None, :],
                 ln1_g=ln1_g[i][None, :], ln1_b=ln1_b[i][None, :], ffn_cw=ffn_conv_w[i], ffn_cb=ffn_conv_b[i][None, :],
                 ln2_g=ln2_g[i][None, :], ln2_b=ln2_b[i][None, :])
        layers.append(p)
    return (_trunk(x_prompt, ln_in_g, ln_in_b, layers), _trunk(x_sample, ln_in_g, ln_in_b, layers))
```

```python
import functools
import math

import jax
import jax.numpy as jnp
import numpy as np
from jax import lax
from jax.experimental import pallas as pl
from jax.experimental.pallas import tpu as pltpu

F32 = jnp.float32
BF16 = jnp.bfloat16

D_MODEL = 1024
DEPTH = 4
HEAD_DIM = 64
N_Q_HEADS = 8
N_KV_HEADS = 2
ATTN_WIDTH = N_Q_HEADS * HEAD_DIM
KV_WIDTH = N_KV_HEADS * HEAD_DIM
WINDOW = 128
ROPE_THETA = 10000.0
SSD_HEADS = 16
SSD_HEAD_DIM = 64
SSD_WIDTH = SSD_HEADS * SSD_HEAD_DIM
SSD_GROUPS = 2
HEADS_PER_GROUP = SSD_HEADS // SSD_GROUPS
GROUP_WIDTH = HEADS_PER_GROUP * SSD_HEAD_DIM
SSD_STATE = 128
SSD_CONV = 5
CHUNK = 128
XBC_WIDTH = SSD_WIDTH + 2 * SSD_GROUPS * SSD_STATE
D_FF = 2816
FFN_CONV = 3
DEEPNORM_ALPHA = (2 * DEPTH) ** 0.25
EPS = 1e-5
MASK_VALUE = -1e30
LOG2E = math.log2(math.e)

LANES = 128
SUBLANES = 8
VMEM_LIMIT = 56 * 1024 * 1024
TOKEN_TILE = 1024
FFN_TOKEN_TILE = 1024
ATTN_BQ = 1024
FF_TILE = 256
FFN_OUT_SLABS = 4
FF_HALO = 16
CONV_PAD = 16


def _dot(a, b):
    return jnp.dot(a, b, preferred_element_type=F32)


def _dot_nt(a, b):
    return lax.dot_general(a, b, (((1,), (1,)), ((), ())), preferred_element_type=F32)


def _layer_norm(x, g, b):
    mu = jnp.mean(x, axis=-1, keepdims=True)
    xc = x - mu
    var = jnp.mean(xc * xc, axis=-1, keepdims=True)
    return xc * lax.rsqrt(var + EPS) * g + b


def _silu(x):
    h = 0.5 * x
    return h + h * jnp.tanh(h)


def _split3(x):
    hi = x.astype(BF16)
    r1 = x - hi.astype(F32)
    mid = r1.astype(BF16)
    lo = (r1 - mid.astype(F32)).astype(BF16)
    return jnp.concatenate([hi, mid, lo], axis=1)


def _inproj_kernel(apply_ln, x_ref, lng_ref, lnb_ref, wqk_ref, wv_ref, wz_ref, wx_ref, wdt_ref,
                   cos_ref, sin_ref, *outs):
    if apply_ln:
        h_ref, q_ref, kv_ref, z_ref, xbc_ref, dt_ref = outs
    else:
        q_ref, kv_ref, z_ref, xbc_ref, dt_ref = outs
    x = x_ref[...]
    if apply_ln:
        x = _layer_norm(x, lng_ref[...], lnb_ref[...])
        h_ref[...] = x
    xb = x.astype(BF16)
    tm = x.shape[0]
    cos = cos_ref[...]
    sin = sin_ref[...]
    lane = lax.broadcasted_iota(jnp.int32, (tm, LANES), 1)
    first_half = (lane % HEAD_DIM) < (HEAD_DIM // 2)
    low_head = lane < HEAD_DIM

    def rope(t):
        swapped = jnp.where(first_half, pltpu.roll(t, LANES - HEAD_DIM // 2, 1), pltpu.roll(t, HEAD_DIM // 2, 1))
        return t * cos + swapped * sin

    def both_halves(t):
        sw = pltpu.roll(t, HEAD_DIM, 1)
        return jnp.where(low_head, t, sw), jnp.where(low_head, sw, t)

    qk = _dot(xb, wqk_ref[...])
    scale = HEAD_DIM ** -0.5 * LOG2E
    for j in range(ATTN_WIDTH // LANES):
        q_ref[:, j * LANES:(j + 1) * LANES] = (rope(qk[:, j * LANES:(j + 1) * LANES]) * scale).astype(BF16)
    k0, k1 = both_halves(rope(qk[:, ATTN_WIDTH:ATTN_WIDTH + KV_WIDTH]))
    v0, v1 = both_halves(_dot(xb, wv_ref[...]))
    kv_ref[:, 0 * LANES:1 * LANES] = k0.astype(BF16)
    kv_ref[:, 1 * LANES:2 * LANES] = k1.astype(BF16)
    kv_ref[:, 2 * LANES:3 * LANES] = v0.astype(BF16)
    kv_ref[:, 3 * LANES:4 * LANES] = v1.astype(BF16)
    z_ref[...] = _dot(xb, wz_ref[...]).astype(BF16)
    xbc_ref[...] = _dot(xb, wx_ref[...]).astype(BF16)
    dt_ref[...] = _dot(xb, wdt_ref[...])


def _inproj(x2d, seq, apply_ln, lng, lnb, wqk, wv, wz, wx, wdt, cos_t, sin_t):
    t = x2d.shape[0]
    tm = TOKEN_TILE
    assert t % tm == 0 and seq % tm == 0
    tiles_per_seq = seq // tm
    row = lambda w: pl.BlockSpec((tm, w), lambda i: (i, 0))
    full = lambda a: pl.BlockSpec(a.shape, lambda i: (0,) * a.ndim)
    tab = pl.BlockSpec((tm, LANES), lambda i: (i % tiles_per_seq, 0))
    out_shape = [jax.ShapeDtypeStruct((t, ATTN_WIDTH), BF16), jax.ShapeDtypeStruct((t, 4 * LANES), BF16),
                 jax.ShapeDtypeStruct((t, SSD_WIDTH), BF16), jax.ShapeDtypeStruct((t, XBC_WIDTH), BF16),
                 jax.ShapeDtypeStruct((t, SSD_GROUPS * LANES), F32)]
    out_specs = [row(ATTN_WIDTH), row(4 * LANES), row(SSD_WIDTH), row(XBC_WIDTH), row(SSD_GROUPS * LANES)]
    if apply_ln:
        out_shape = [jax.ShapeDtypeStruct((t, D_MODEL), F32)] + out_shape
        out_specs = [row(D_MODEL)] + out_specs
    return pl.pallas_call(
        functools.partial(_inproj_kernel, apply_ln),
        grid=(t // tm,),
        in_specs=[row(D_MODEL), full(lng), full(lnb), full(wqk), full(wv), full(wz), full(wx), full(wdt), tab, tab],
        out_specs=out_specs,
        out_shape=out_shape,
        compiler_params=pltpu.CompilerParams(dimension_semantics=("arbitrary",), vmem_limit_bytes=VMEM_LIMIT),
        name="in_proj_ln" if apply_ln else "in_proj",
    )(x2d, lng, lnb, wqk, wv, wz, wx, wdt, cos_t, sin_t)


def _attn_kernel(seq, sink_ref, q_ref, kl_ref, km_ref, kr_ref, vl_ref, vm_ref, vr_ref, o_ref):
    bq = q_ref.shape[1]
    sub = WINDOW
    ks = sub + 2 * WINDOW
    start = pl.program_id(1) * bq
    qi = lax.broadcasted_iota(jnp.int32, (sub, ks), 0)
    kc = lax.broadcasted_iota(jnp.int32, (sub, ks), 1) - WINDOW
    band = jnp.abs(kc - qi) <= WINDOW
    lane = lax.broadcasted_iota(jnp.int32, (sub, LANES), 1)
    low_head = lane < HEAD_DIM
    group = N_Q_HEADS // N_KV_HEADS
    for h in range(N_KV_HEADS):
        ksl = slice(h * LANES, (h + 1) * LANES)
        kk = jnp.concatenate([kl_ref[0, :, ksl], km_ref[0, :, ksl], kr_ref[0, :, ksl]], axis=0)
        vv = jnp.concatenate([vl_ref[0, :, ksl], vm_ref[0, :, ksl], vr_ref[0, :, ksl]], axis=0)
        for sb in range(bq // sub):
            rows = slice(sb * sub, (sb + 1) * sub)
            kpos = kc + (start + sb * sub)
            allow = band & (kpos >= 0) & (kpos < seq)
            qs = []
            for r in range(group):
                head = group * h + r
                qt = q_ref[0, rows, (head // 2) * LANES:(head // 2 + 1) * LANES]
                keep = low_head if head % 2 == 0 else jnp.logical_not(low_head)
                qs.append(jnp.where(keep, qt, jnp.zeros_like(qt)))
            sc = _dot_nt(jnp.concatenate(qs, axis=0), kk[sb * sub:sb * sub + ks])
            ps, inv = [], []
            for r in range(group):
                sink = sink_ref[group * h + r] * LOG2E
                s = sc[r * sub:(r + 1) * sub]
                s = jnp.concatenate([jnp.where(allow[:, :WINDOW], s[:, :WINDOW], MASK_VALUE), s[:, WINDOW:ks - WINDOW],
                                     jnp.where(allow[:, ks - WINDOW:], s[:, ks - WINDOW:], MASK_VALUE)], axis=1)
                m = jnp.maximum(jnp.max(s, axis=-1, keepdims=True), sink)
                p = jnp.exp2(s - m)
                denom = jnp.sum(p, axis=-1, keepdims=True) + jnp.exp2(sink - m)
                ps.append(p.astype(BF16))
                inv.append(1.0 / denom)
            pv = _dot(jnp.concatenate(ps, axis=0), vv[sb * sub:sb * sub + ks])
            for pair in range(group // 2):
                even = pv[(2 * pair) * sub:(2 * pair + 1) * sub] * inv[2 * pair]
                odd = pv[(2 * pair + 1) * sub:(2 * pair + 2) * sub] * inv[2 * pair + 1]
                col = (group * h) // 2 + pair
                o_ref[0, rows, col * LANES:(col + 1) * LANES] = jnp.where(low_head, even, odd).astype(BF16)


def _attention(q, kv, sink):
    b, seq, _ = q.shape
    bq = ATTN_BQ
    assert seq % bq == 0 and bq % WINDOW == 0
    r = bq // WINDOW
    nwb = seq // WINDOW
    side = lambda lane_blk, f: pl.BlockSpec((1, WINDOW, 2 * LANES), lambda bi, i: (bi, f(i), lane_blk))
    mid = lambda lane_blk: pl.BlockSpec((1, bq, 2 * LANES), lambda bi, i: (bi, i, lane_blk))
    left = lambda i: jnp.maximum(i * r - 1, 0)
    right = lambda i: jnp.minimum((i + 1) * r, nwb - 1)
    return pl.pallas_call(
        functools.partial(_attn_kernel, seq),
        grid=(b, seq // bq),
        in_specs=[pl.BlockSpec(memory_space=pltpu.SMEM),
                  pl.BlockSpec((1, bq, ATTN_WIDTH), lambda bi, i: (bi, i, 0)),
                  side(0, left), mid(0), side(0, right), side(1, left), mid(1), side(1, right)],
        out_specs=pl.BlockSpec((1, bq, ATTN_WIDTH), lambda bi, i: (bi, i, 0)),
        out_shape=jax.ShapeDtypeStruct((b, seq, ATTN_WIDTH), BF16),
        compiler_params=pltpu.CompilerParams(dimension_semantics=("arbitrary", "arbitrary"),
                                             vmem_limit_bytes=VMEM_LIMIT),
        name="window_attn",
    )(sink, q, kv, kv, kv, kv, kv, kv)


def _ssd_kernel(x_ref, bm_ref, cm_ref, z_ref, dt_ref, cwx_ref, cwb_ref, cwc_ref, cbx_ref, cbb_ref, cbc_ref,
                gp_ref, dexp_ref, ng_ref, tril_ref, triu_ref, ef_ref, eb_ref, sh_ref, y_ref, ss_ref,
                pad_sc, xs_sc, bs_sc, cs_sc, bt_sc, yacc_sc, locb_sc, ef_sc, er_sc, keepf_sc, keepb_sc, fr_sc, rows_sc,
                sf_sc, sb_sc):
    seq = x_ref.shape[1]
    nchunks = seq // CHUNK
    gw = GROUP_WIDTH
    hp = HEADS_PER_GROUP
    pad = CONV_PAD
    taps = [k for k in range(SSD_CONV) if k != SSD_CONV // 2]

    zeros = jnp.zeros((pad, gw + 2 * LANES), BF16)
    pad_sc[0:pad, :] = zeros
    pad_sc[pad + seq:pad + seq + pad, :] = zeros

    def fill(c, carry):
        r0 = pl.multiple_of(c * CHUNK, CHUNK)
        pad_sc[pl.ds(pad + r0, CHUNK), 0:gw] = x_ref[0, pl.ds(r0, CHUNK), :]
        pad_sc[pl.ds(pad + r0, CHUNK), gw:gw + LANES] = bm_ref[0, pl.ds(r0, CHUNK), :]
        pad_sc[pl.ds(pad + r0, CHUNK), gw + LANES:gw + 2 * LANES] = cm_ref[0, pl.ds(r0, CHUNK), :]
        return carry

    lax.fori_loop(0, nchunks, fill, 0)

    lane_row = lax.broadcasted_iota(jnp.int32, (1, LANES), 1)
    dt_bias = gp_ref[0, 0:1, :]
    a_row = jnp.where(lane_row < 2 * hp, -jnp.exp(gp_ref[0, 1:2, :]), 0.0)
    dexp = dexp_ref[...]
    tril = tril_ref[...]
    triu = triu_ref[...]
    qi = lax.broadcasted_iota(jnp.int32, (CHUNK, CHUNK), 0)
    si = lax.broadcasted_iota(jnp.int32, (CHUNK, CHUNK), 1)
    lower = si <= qi
    before = si < qi
    after = si > qi
    low_head = si < SSD_HEAD_DIM
    fwd_lane = si < hp

    def blockdiag(t):
        return jnp.concatenate([jnp.where(low_head, t, 0.0), jnp.where(low_head, 0.0, t)], axis=0).astype(BF16)

    def head_rows(v, e_ref):
        return _dot(_split3(jnp.broadcast_to(v, (SUBLANES, LANES))), e_ref[...])

    def chunk_scalars(r0):
        x = dt_ref[0, pl.ds(r0, CHUNK), :] + dt_bias
        dt = jnp.maximum(x, 0.0) + jnp.log1p(jnp.exp(-jnp.abs(x)))
        pieces = _split3(dt * a_row)
        fp = _dot(tril, pieces)
        rp = _dot(triu, pieces)
        f = fp[:, 0:LANES] + fp[:, LANES:2 * LANES] + fp[:, 2 * LANES:3 * LANES]
        r = rp[:, 0:LANES] + rp[:, LANES:2 * LANES] + rp[:, 2 * LANES:3 * LANES]
        return dt, f, r

    def expand(v, first):
        tiles = [jnp.where(low_head, v[:, first + 2 * p:first + 2 * p + 1], v[:, first + 2 * p + 1:first + 2 * p + 2])
                 for p in range(hp // 2)]
        return jnp.concatenate(tiles, axis=1)

    def conv(c, carry):
        r0 = pl.multiple_of(c * CHUNK, CHUNK)
        shifted = _dot(sh_ref[...], pad_sc[pl.ds(r0, CHUNK + 2 * pad), :])
        centre = pad_sc[pl.ds(r0 + pad, CHUNK), :].astype(F32)

        def one(lo, hi, w_ref, b_ref):
            acc = b_ref[...] + w_ref[SSD_CONV // 2:SSD_CONV // 2 + 1, :] * centre[:, lo:hi]
            for j, k in enumerate(taps):
                acc = acc + w_ref[k:k + 1, :] * shifted[j * CHUNK:(j + 1) * CHUNK, lo:hi]
            return _silu(acc)

        xs_sc[pl.ds(r0, CHUNK), :] = one(0, gw, cwx_ref, cbx_ref)
        bconv = one(gw, gw + LANES, cwb_ref, cbb_ref)
        bs_sc[pl.ds(r0, CHUNK), :] = bconv.astype(BF16)
        bt_sc[c] = bconv.T
        cs_sc[pl.ds(r0, CHUNK), :] = one(gw + LANES, gw + 2 * LANES, cwc_ref, cbc_ref).astype(BF16)

        dt, f, r = chunk_scalars(r0)
        fr = jnp.where(fwd_lane, f, r)
        fr_sc[c] = fr * LOG2E
        pt = jnp.where(si < 2 * hp, fr, dt).T
        cums_t = pt[0:2 * hp, :]
        dt_t = pt[2 * hp:4 * hp, :]
        dte_t = jnp.exp(jnp.where(qi[0:2 * hp] < hp, cums_t[:, CHUNK - 1:CHUNK] - cums_t, cums_t[:, 0:1] - cums_t)) * dt_t
        rows_sc[c, 0:2 * hp, :] = cums_t * LOG2E
        rows_sc[c, 2 * hp:4 * hp, :] = dt_t
        rows_sc[c, 4 * hp:6 * hp, :] = dte_t
        keepf_sc[c] = head_rows(jnp.exp(f[CHUNK - 1:CHUNK, :]), ef_ref)
        keepb_sc[c] = head_rows(jnp.exp(r[0:1, :]), eb_ref)
        ef_sc[c] = _dot(_split3(jnp.exp(f)), ef_ref[...])
        er_sc[c] = expand(jnp.exp(r), hp)
        return carry

    lax.fori_loop(0, nchunks, conv, 0, unroll=8)

    sf_sc[...] = jnp.zeros_like(sf_sc)
    sb_sc[...] = jnp.zeros_like(sb_sc)

    def ascend(c, carry):
        r0 = pl.multiple_of(c * CHUNK, CHUNK)
        fr = fr_sc[c]
        cums_t = rows_sc[c, 0:2 * hp, :]
        dt_t = rows_sc[c, 2 * hp:4 * hp, :]
        dte_t = rows_sc[c, 4 * hp:6 * hp, :]
        keep_f = keepf_sc[c]
        xs = xs_sc[pl.ds(r0, CHUNK), :]
        cc = cs_sc[pl.ds(r0, CHUNK), :]
        cb = _dot_nt(cc, bs_sc[pl.ds(r0, CHUNK), :])
        yoff = _dot(cc, sf_sc[...].astype(BF16)) * ef_sc[c]
        bt = bt_sc[c]
        for pair in range(hp // 2):
            ms, bfs, bbs = [], [], []
            for i in (2 * pair, 2 * pair + 1):
                seg = jnp.where(lower, fr[:, i:i + 1] - cums_t[i:i + 1, :], fr[:, hp + i:hp + i + 1] - cums_t[hp + i:hp + i + 1, :])
                dtf_row = dt_t[i:i + 1, :]
                dtb_row = dt_t[hp + i:hp + i + 1, :]
                w = jnp.where(before, dtf_row, jnp.where(after, dtb_row, dtf_row + dtb_row))
                ms.append((cb * jnp.exp2(seg) * w).astype(BF16))
                bfs.append((bt * dte_t[i:i + 1, :]).astype(BF16))
                bbs.append((bt * dte_t[hp + i:hp + i + 1, :]).astype(BF16))
            sl = slice(pair * LANES, (pair + 1) * LANES)
            lhs = jnp.concatenate([jnp.concatenate(ms, axis=1), jnp.concatenate(bfs, axis=1),
                                   jnp.concatenate(bbs, axis=1)], axis=0)
            res = _dot(lhs, blockdiag(xs[:, sl]))
            yacc_sc[pl.ds(r0, CHUNK), sl] = res[0:CHUNK] + yoff[:, sl] + xs[:, sl] * dexp[:, sl]
            sf_sc[:, sl] = sf_sc[:, sl] * keep_f[0:1, sl] + res[CHUNK:2 * CHUNK]
            locb_sc[c, :, sl] = res[2 * CHUNK:3 * CHUNK]
        return carry

    lax.fori_loop(0, nchunks, ascend, 0, unroll=8)

    def descend(j, carry):
        c = nchunks - 1 - j
        r0 = pl.multiple_of(c * CHUNK, CHUNK)
        sb = sb_sc[...]
        y = yacc_sc[pl.ds(r0, CHUNK), :] + _dot(cs_sc[pl.ds(r0, CHUNK), :], sb.astype(BF16)) * er_sc[c]
        sb_sc[...] = sb * keepb_sc[c][0:1, :] + locb_sc[c]
        gated = y * _silu(z_ref[0, pl.ds(r0, CHUNK), :].astype(F32))
        ss = jnp.sum(gated * gated, axis=1, keepdims=True)
        ss_ref[0, pl.ds(r0, CHUNK), :] = jnp.broadcast_to(ss, (CHUNK, LANES))
        y_ref[0, pl.ds(r0, CHUNK), :] = (gated * ng_ref[...]).astype(BF16)
        return carry

    lax.fori_loop(0, nchunks, descend, 0, unroll=8)


def _ssd_constants():
    idx = np.arange(CHUNK)
    tril = (idx[None, :] <= idx[:, None]).astype(np.float32)
    triu = (idx[None, :] >= idx[:, None]).astype(np.float32)
    rows = np.arange(3 * LANES)[:, None] % LANES
    cols = np.arange(GROUP_WIDTH)[None, :] // SSD_HEAD_DIM
    ef = (rows == cols).astype(np.float32)
    eb = (rows == cols + HEADS_PER_GROUP).astype(np.float32)
    taps = [k for k in range(SSD_CONV) if k != SSD_CONV // 2]
    src = np.arange(CHUNK + 2 * CONV_PAD)[None, :]
    shift = np.concatenate([(src == idx[:, None] + CONV_PAD + k - SSD_CONV // 2) for k in taps], axis=0).astype(np.float32)
    return tuple(jnp.asarray(a, dtype=BF16) for a in (tril, triu, ef, eb, shift))


def _ssd(xbc, z, dt, conv_w, conv_b, gp, d_exp, norm_g):
    b, seq, _ = xbc.shape
    assert seq % CHUNK == 0
    gw = GROUP_WIDTH
    nx = SSD_WIDTH // LANES
    nchunks = seq // CHUNK
    consts = _ssd_constants()
    seq_blk = lambda w, f: pl.BlockSpec((1, seq, w), lambda bi, g: (bi, 0, f(g)))
    par = lambda rows, w, f: pl.BlockSpec((rows, w), lambda bi, g: (0, f(g)))
    full = lambda a: pl.BlockSpec(a.shape, lambda bi, g: (0,) * a.ndim)
    return pl.pallas_call(
        _ssd_kernel,
        grid=(b, SSD_GROUPS),
        in_specs=[seq_blk(gw, lambda g: g), seq_blk(LANES, lambda g: nx + g), seq_blk(LANES, lambda g: nx + SSD_GROUPS + g),
                  seq_blk(gw, lambda g: g), seq_blk(LANES, lambda g: g),
                  par(SSD_CONV, gw, lambda g: g), par(SSD_CONV, LANES, lambda g: nx + g),
                  par(SSD_CONV, LANES, lambda g: nx + SSD_GROUPS + g),
                  par(1, gw, lambda g: g), par(1, LANES, lambda g: nx + g), par(1, LANES, lambda g: nx + SSD_GROUPS + g),
                  pl.BlockSpec((1, SUBLANES, LANES), lambda bi, g: (g, 0, 0)), par(1, gw, lambda g: g), par(1, gw, lambda g: g)]
                 + [full(a) for a in consts],
        out_specs=[seq_blk(gw, lambda g: g), seq_blk(LANES, lambda g: g)],
        out_shape=[jax.ShapeDtypeStruct((b, seq, SSD_WIDTH), BF16), jax.ShapeDtypeStruct((b, seq, SSD_GROUPS * LANES), F32)],
        scratch_shapes=[pltpu.VMEM((seq + 2 * CONV_PAD, gw + 2 * LANES), BF16), pltpu.VMEM((seq, gw), F32),
                        pltpu.VMEM((seq, LANES), BF16), pltpu.VMEM((seq, LANES), BF16),
                        pltpu.VMEM((nchunks, SSD_STATE, CHUNK), F32), pltpu.VMEM((seq, gw), F32),
                        pltpu.VMEM((nchunks, SSD_STATE, gw), F32), pltpu.VMEM((nchunks, CHUNK, gw), F32),
                        pltpu.VMEM((nchunks, CHUNK, gw), F32), pltpu.VMEM((nchunks, SUBLANES, gw), F32), pltpu.VMEM((nchunks, SUBLANES, gw), F32),
                        pltpu.VMEM((nchunks, CHUNK, LANES), F32), pltpu.VMEM((nchunks, 6 * HEADS_PER_GROUP, LANES), F32),
                        pltpu.VMEM((SSD_STATE, gw), F32), pltpu.VMEM((SSD_STATE, gw), F32)],
        compiler_params=pltpu.CompilerParams(dimension_semantics=("arbitrary", "arbitrary"),
                                             vmem_limit_bytes=VMEM_LIMIT),
        name="bidir_ssd",
    )(xbc, xbc, xbc, z, dt, conv_w, conv_w, conv_w, conv_b, conv_b, conv_b, gp, d_exp, norm_g, *consts)


def _outproj_kernel(attn_ref, y_ref, ss_ref, h_ref, wa_ref, ws_ref, g_ref, b_ref, o_ref):
    slab = h_ref.shape[0] // FFN_OUT_SLABS
    for s in range(FFN_OUT_SLABS):
        rows = slice(s * slab, (s + 1) * slab)
        total = ss_ref[rows, 0:1]
        for g in range(1, SSD_GROUPS):
            total = total + ss_ref[rows, g * LANES:g * LANES + 1]
        rs = lax.rsqrt(total * (1.0 / SSD_WIDTH) + EPS)
        mix = _dot(attn_ref[rows, :], wa_ref[...]) + rs * _dot(y_ref[rows, :], ws_ref[...])
        o_ref[rows, :] = _layer_norm(DEEPNORM_ALPHA * h_ref[rows, :] + mix, g_ref[...], b_ref[...])


def _outproj(attn, y, ss, h, wa, ws, g, b):
    t = h.shape[0]
    tm = TOKEN_TILE
    row = lambda w: pl.BlockSpec((tm, w), lambda i: (i, 0))
    full = lambda a: pl.BlockSpec(a.shape, lambda i: (0,) * a.ndim)
    return pl.pallas_call(
        _outproj_kernel,
        grid=(t // tm,),
        in_specs=[row(ATTN_WIDTH), row(SSD_WIDTH), row(SSD_GROUPS * LANES), row(D_MODEL), full(wa), full(ws), full(g), full(b)],
        out_specs=row(D_MODEL),
        out_shape=jax.ShapeDtypeStruct((t, D_MODEL), F32),
        compiler_params=pltpu.CompilerParams(dimension_semantics=("arbitrary",), vmem_limit_bytes=VMEM_LIMIT),
        name="out_proj_ln",
    )(attn, y, ss, h, wa, ws, g, b)


def _ffn_kernel(tiles_per_seq, hp_ref, h_ref, hn_ref, wup_ref, cw_ref, cb_ref, wdn_ref, g_ref, b_ref, o_ref,
                hb_sc, act_sc):
    tm = h_ref.shape[0]
    pos = pl.program_id(0) % tiles_per_seq
    h = h_ref[...]
    hb_sc[0:FF_HALO, :] = jnp.where(pos == 0, 0.0, hp_ref[...]).astype(BF16)
    hb_sc[FF_HALO:FF_HALO + tm, :] = h.astype(BF16)
    hb_sc[FF_HALO + tm:, :] = jnp.where(pos == tiles_per_seq - 1, 0.0, hn_ref[...]).astype(BF16)
    for j in range(D_FF // FF_TILE):
        cols = slice(j * FF_TILE, (j + 1) * FF_TILE)
        vcols = slice(D_FF + j * FF_TILE, D_FF + (j + 1) * FF_TILE)
        g_ext = _dot(hb_sc[...], wup_ref[:, cols])
        val = _dot(hb_sc[FF_HALO:FF_HALO + tm, :], wup_ref[:, vcols])
        gate = cb_ref[:, cols] + cw_ref[1:2, cols] * g_ext[FF_HALO:FF_HALO + tm]
        gate = gate + cw_ref[0:1, cols] * g_ext[FF_HALO - 1:FF_HALO - 1 + tm]
        gate = gate + cw_ref[2:3, cols] * g_ext[FF_HALO + 1:FF_HALO + 1 + tm]
        act_sc[:, cols] = (_silu(gate) * val).astype(BF16)
    slab = tm // FFN_OUT_SLABS
    for s in range(FFN_OUT_SLABS):
        rows = slice(s * slab, (s + 1) * slab)
        ffn = _dot(act_sc[rows, :], wdn_ref[...])
        o_ref[rows, :] = _layer_norm(DEEPNORM_ALPHA * h_ref[rows, :] + ffn, g_ref[...], b_ref[...])


def _ffn(h, seq, wup, cw, cb, wdn, g, b):
    t = h.shape[0]
    tm = FFN_TOKEN_TILE
    assert seq % tm == 0 and tm % FF_HALO == 0 and D_FF % FF_TILE == 0
    tiles_per_seq = seq // tm
    hb = tm // FF_HALO
    nhb = t // FF_HALO
    row = pl.BlockSpec((tm, D_MODEL), lambda i: (i, 0))
    prev = pl.BlockSpec((FF_HALO, D_MODEL), lambda i: (jnp.maximum(i * hb - 1, 0), 0))
    nxt = pl.BlockSpec((FF_HALO, D_MODEL), lambda i: (jnp.minimum((i + 1) * hb, nhb - 1), 0))
    full = lambda a: pl.BlockSpec(a.shape, lambda i: (0,) * a.ndim, pipeline_mode=pl.Buffered(1))
    return pl.pallas_call(
        functools.partial(_ffn_kernel, tiles_per_seq),
        grid=(t // tm,),
        in_specs=[prev, row, nxt, full(wup), full(cw), full(cb), full(wdn), full(g), full(b)],
        out_specs=row,
        out_shape=jax.ShapeDtypeStruct((t, D_MODEL), F32),
        scratch_shapes=[pltpu.VMEM((tm + 2 * FF_HALO, D_MODEL), BF16), pltpu.VMEM((tm, D_FF), BF16)],
        compiler_params=pltpu.CompilerParams(dimension_semantics=("arbitrary",), vmem_limit_bytes=VMEM_LIMIT),
        name="ffn_ln",
    )(h, h, h, wup, cw, cb, wdn, g, b)


def _rope_tables(seq):
    inv_freq = ROPE_THETA ** (-jnp.arange(0, HEAD_DIM, 2, dtype=F32) / HEAD_DIM)
    ang = jnp.arange(seq, dtype=F32)[:, None] * inv_freq[None, :]
    cos, sin = jnp.cos(ang), jnp.sin(ang)
    reps = LANES // HEAD_DIM
    return jnp.tile(jnp.concatenate([cos, cos], axis=1), (1, reps)), jnp.tile(jnp.concatenate([-sin, sin], axis=1), (1, reps))


def _per_group(fwd, bwd):
    hp = HEADS_PER_GROUP
    out = jnp.zeros((SSD_GROUPS, LANES), F32)
    for g in range(SSD_GROUPS):
        out = out.at[g, 0:hp].set(fwd[g * hp:(g + 1) * hp])
        out = out.at[g, hp:2 * hp].set(bwd[g * hp:(g + 1) * hp])
    return out


def _layer_params(w_in, ssd_a_log, ssd_dt_bias, ssd_d, w_out, w_up, w_down):
    o_k = ATTN_WIDTH + KV_WIDTH
    o_z = o_k + KV_WIDTH
    o_x = o_z + SSD_WIDTH
    o_dt = o_x + XBC_WIDTH
    hp = HEADS_PER_GROUP
    wdt_src = w_in[:, o_dt:]
    wdt = jnp.zeros((D_MODEL, SSD_GROUPS * LANES), F32)
    for g in range(SSD_GROUPS):
        wdt = wdt.at[:, g * LANES:g * LANES + hp].set(wdt_src[:, g * hp:(g + 1) * hp])
        wdt = wdt.at[:, g * LANES + hp:g * LANES + 2 * hp].set(wdt_src[:, SSD_HEADS + g * hp:SSD_HEADS + (g + 1) * hp])
        wdt = wdt.at[:, g * LANES + 2 * hp:g * LANES + 4 * hp].set(wdt[:, g * LANES:g * LANES + 2 * hp])
    gp = jnp.zeros((SSD_GROUPS, SUBLANES, LANES), F32)
    dt_bias = _per_group(ssd_dt_bias[0], ssd_dt_bias[1])
    gp = gp.at[:, 0, :].set(dt_bias.at[:, 2 * hp:4 * hp].set(dt_bias[:, 0:2 * hp]))
    gp = gp.at[:, 1, :].set(_per_group(ssd_a_log[0], ssd_a_log[1]))
    return dict(
        wqk=w_in[:, :o_k].astype(BF16), wv=w_in[:, o_k:o_z].astype(BF16), wz=w_in[:, o_z:o_x].astype(BF16),
        wx=w_in[:, o_x:o_dt].astype(BF16), wdt=wdt.astype(BF16), gp=gp,
        d_exp=jnp.repeat(ssd_d, SSD_HEAD_DIM)[None, :],
        wa=w_out[:ATTN_WIDTH].astype(BF16), ws=w_out[ATTN_WIDTH:].astype(BF16),
        wup=w_up.astype(BF16), wdn=w_down.astype(BF16))


def _trunk(x, ln_in_g, ln_in_b, layers):
    b, seq, _ = x.shape
    cos_t, sin_t = _rope_tables(seq)
    h = x.reshape(b * seq, D_MODEL)
    lng, lnb = ln_in_g[None, :], ln_in_b[None, :]
    for li, p in enumerate(layers):
        outs = _inproj(h, seq, li == 0, lng, lnb, p["wqk"], p["wv"], p["wz"], p["wx"], p["wdt"], cos_t, sin_t)
        if li == 0:
            h, outs = outs[0], outs[1:]
        q, kv, z, xbc, dt = outs
        three = lambda a: a.reshape(b, seq, a.shape[-1])
        attn = _attention(three(q), three(kv), p["sink"])
        y, ss = _ssd(three(xbc), three(z), three(dt), p["conv_w"], p["conv_b"], p["gp"], p["d_exp"], p["norm_g"])
        h = _outproj(attn.reshape(b * seq, -1), y.reshape(b * seq, -1), ss.reshape(b * seq, -1), h,
                     p["wa"], p["ws"], p["ln1_g"], p["ln1_b"])
        h = _ffn(h, seq, p["wup"], p["ffn_cw"], p["ffn_cb"], p["wdn"], p["ln2_g"], p["ln2_b"])
    return h.reshape(b, seq, D_MODEL)


def kernel(x_prompt, x_sample, ln_in_g, ln_in_b, w_in, attn_sink, ssd_conv_w, ssd_conv_b, ssd_a_log, ssd_dt_bias, ssd_d, ssd_norm_g, w_out, ln1_g, ln1_b, w_up, ffn_conv_w, ffn_conv_b, w_down, ln2_g, ln2_b):
    layers = []
    for i in range(w_in.shape[0]):
        p = _layer_params(w_in[i], ssd_a_log[i], ssd_dt_bias[i], ssd_d[i], w_out[i], w_up[i], w_down[i])
        p.update(sink=attn_sink[i], conv_w=ssd_conv_w[i], conv_b=ssd_conv_b[i][None, :], norm_g=ssd_norm_g[i][None, :],
                 ln1_g=ln1_g[i][None, :], ln1_b=ln1_b[i][None, :], ffn_cw=ffn_conv_w[i], ffn_cb=ffn_conv_b[i][None, :],
                 ln2_g=ln2_g[i][None, :], ln2_b=ln2_b[i][None, :])
        layers.append(p)
    return (_trunk(x_prompt, ln_in_g, ln_in_b, layers), _trunk(x_sample, ln_in_g, ln_in_b, layers))
```

```python
import functools
import math

import jax
import jax.numpy as jnp
import numpy as np
from jax import lax
from jax.experimental import pallas as pl
from jax.experimental.pallas import tpu as pltpu

F32 = jnp.float32
BF16 = jnp.bfloat16

D_MODEL = 1024
DEPTH = 4
HEAD_DIM = 64
N_Q_HEADS = 8
N_KV_HEADS = 2
ATTN_WIDTH = N_Q_HEADS * HEAD_DIM
KV_WIDTH = N_KV_HEADS * HEAD_DIM
WINDOW = 128
ROPE_THETA = 10000.0
SSD_HEADS = 16
SSD_HEAD_DIM = 64
SSD_WIDTH = SSD_HEADS * SSD_HEAD_DIM
SSD_GROUPS = 2
HEADS_PER_GROUP = SSD_HEADS // SSD_GROUPS
GROUP_WIDTH = HEADS_PER_GROUP * SSD_HEAD_DIM
SSD_STATE = 128
SSD_CONV = 5
CHUNK = 128
XBC_WIDTH = SSD_WIDTH + 2 * SSD_GROUPS * SSD_STATE
D_FF = 2816
FFN_CONV = 3
DEEPNORM_ALPHA = (2 * DEPTH) ** 0.25
EPS = 1e-5
MASK_VALUE = -1e30
LOG2E = math.log2(math.e)

LANES = 128
SUBLANES = 8
VMEM_LIMIT = 56 * 1024 * 1024
TOKEN_TILE = 1024
FFN_TOKEN_TILE = 1024
ATTN_BQ = 1024
FF_TILE = 256
FFN_OUT_SLABS = 4
FF_HALO = 16
CONV_PAD = 16


def _dot(a, b):
    return jnp.dot(a, b, preferred_element_type=F32)


def _dot_nt(a, b):
    return lax.dot_general(a, b, (((1,), (1,)), ((), ())), preferred_element_type=F32)


def _layer_norm(x, g, b):
    mu = jnp.mean(x, axis=-1, keepdims=True)
    xc = x - mu
    var = jnp.mean(xc * xc, axis=-1, keepdims=True)
    return xc * lax.rsqrt(var + EPS) * g + b


def _silu(x):
    h = 0.5 * x
    return h + h * jnp.tanh(h)


def _split3(x):
    hi = x.astype(BF16)
    r1 = x - hi.astype(F32)
    mid = r1.astype(BF16)
    lo = (r1 - mid.astype(F32)).astype(BF16)
    return jnp.concatenate([hi, mid, lo], axis=1)


def _inproj_kernel(apply_ln, x_ref, lng_ref, lnb_ref, wqk_ref, wv_ref, wz_ref, wx_ref, wdt_ref,
                   cos_ref, sin_ref, *outs):
    if apply_ln:
        h_ref, q_ref, kv_ref, z_ref, xbc_ref, dt_ref = outs
    else:
        q_ref, kv_ref, z_ref, xbc_ref, dt_ref = outs
    x = x_ref[...]
    if apply_ln:
        x = _layer_norm(x, lng_ref[...], lnb_ref[...])
        h_ref[...] = x
    xb = x.astype(BF16)
    tm = x.shape[0]
    cos = cos_ref[...]
    sin = sin_ref[...]
    lane = lax.broadcasted_iota(jnp.int32, (tm, LANES), 1)
    first_half = (lane % HEAD_DIM) < (HEAD_DIM // 2)
    low_head = lane < HEAD_DIM

    def rope(t):
        swapped = jnp.where(first_half, pltpu.roll(t, LANES - HEAD_DIM // 2, 1), pltpu.roll(t, HEAD_DIM // 2, 1))
        return t * cos + swapped * sin

    def both_halves(t):
        sw = pltpu.roll(t, HEAD_DIM, 1)
        return jnp.where(low_head, t, sw), jnp.where(low_head, sw, t)

    qk = _dot(xb, wqk_ref[...])
    scale = HEAD_DIM ** -0.5 * LOG2E
    for j in range(ATTN_WIDTH // LANES):
        q_ref[:, j * LANES:(j + 1) * LANES] = (rope(qk[:, j * LANES:(j + 1) * LANES]) * scale).astype(BF16)
    k0, k1 = both_halves(rope(qk[:, ATTN_WIDTH:ATTN_WIDTH + KV_WIDTH]))
    v0, v1 = both_halves(_dot(xb, wv_ref[...]))
    kv_ref[:, 0 * LANES:1 * LANES] = k0.astype(BF16)
    kv_ref[:, 1 * LANES:2 * LANES] = k1.astype(BF16)
    kv_ref[:, 2 * LANES:3 * LANES] = v0.astype(BF16)
    kv_ref[:, 3 * LANES:4 * LANES] = v1.astype(BF16)
    z_ref[...] = _dot(xb, wz_ref[...]).astype(BF16)
    xbc_ref[...] = _dot(xb, wx_ref[...]).astype(BF16)
    dt_ref[...] = _dot(xb, wdt_ref[...])


def _inproj(x2d, seq, apply_ln, lng, lnb, wqk, wv, wz, wx, wdt, cos_t, sin_t):
    t = x2d.shape[0]
    tm = TOKEN_TILE
    assert t % tm == 0 and seq % tm == 0
    tiles_per_seq = seq // tm
    row = lambda w: pl.BlockSpec((tm, w), lambda i: (i, 0))
    full = lambda a: pl.BlockSpec(a.shape, lambda i: (0,) * a.ndim)
    tab = pl.BlockSpec((tm, LANES), lambda i: (i % tiles_per_seq, 0))
    out_shape = [jax.ShapeDtypeStruct((t, ATTN_WIDTH), BF16), jax.ShapeDtypeStruct((t, 4 * LANES), BF16),
                 jax.ShapeDtypeStruct((t, SSD_WIDTH), BF16), jax.ShapeDtypeStruct((t, XBC_WIDTH), BF16),
                 jax.ShapeDtypeStruct((t, SSD_GROUPS * LANES), F32)]
    out_specs = [row(ATTN_WIDTH), row(4 * LANES), row(SSD_WIDTH), row(XBC_WIDTH), row(SSD_GROUPS * LANES)]
    if apply_ln:
        out_shape = [jax.ShapeDtypeStruct((t, D_MODEL), F32)] + out_shape
        out_specs = [row(D_MODEL)] + out_specs
    return pl.pallas_call(
        functools.partial(_inproj_kernel, apply_ln),
        grid=(t // tm,),
        in_specs=[row(D_MODEL), full(lng), full(lnb), full(wqk), full(wv), full(wz), full(wx), full(wdt), tab, tab],
        out_specs=out_specs,
        out_shape=out_shape,
        compiler_params=pltpu.CompilerParams(dimension_semantics=("arbitrary",), vmem_limit_bytes=VMEM_LIMIT),
        name="in_proj_ln" if apply_ln else "in_proj",
    )(x2d, lng, lnb, wqk, wv, wz, wx, wdt, cos_t, sin_t)


def _attn_kernel(seq, sink_ref, q_ref, kl_ref, km_ref, kr_ref, vl_ref, vm_ref, vr_ref, o_ref):
    bq = q_ref.shape[1]
    sub = WINDOW
    ks = sub + 2 * WINDOW
    start = pl.program_id(1) * bq
    qi = lax.broadcasted_iota(jnp.int32, (sub, ks), 0)
    kc = lax.broadcasted_iota(jnp.int32, (sub, ks), 1) - WINDOW
    band = jnp.abs(kc - qi) <= WINDOW
    lane = lax.broadcasted_iota(jnp.int32, (sub, LANES), 1)
    low_head = lane < HEAD_DIM
    group = N_Q_HEADS // N_KV_HEADS
    for h in range(N_KV_HEADS):
        ksl = slice(h * LANES, (h + 1) * LANES)
        kk = jnp.concatenate([kl_ref[0, :, ksl], km_ref[0, :, ksl], kr_ref[0, :, ksl]], axis=0)
        vv = jnp.concatenate([vl_ref[0, :, ksl], vm_ref[0, :, ksl], vr_ref[0, :, ksl]], axis=0)
        for sb in range(bq // sub):
            rows = slice(sb * sub, (sb + 1) * sub)
            kpos = kc + (start + sb * sub)
            allow = band & (kpos >= 0) & (kpos < seq)
            qs = []
            for r in range(group):
                head = group * h + r
                qt = q_ref[0, rows, (head // 2) * LANES:(head // 2 + 1) * LANES]
                keep = low_head if head % 2 == 0 else jnp.logical_not(low_head)
                qs.append(jnp.where(keep, qt, jnp.zeros_like(qt)))
            sc = _dot_nt(jnp.concatenate(qs, axis=0), kk[sb * sub:sb * sub + ks])
            ps, inv = [], []
            for r in range(group):
                sink = sink_ref[group * h + r] * LOG2E
                s = sc[r * sub:(r + 1) * sub]
                s = jnp.concatenate([jnp.where(allow[:, :WINDOW], s[:, :WINDOW], MASK_VALUE), s[:, WINDOW:ks - WINDOW],
                                     jnp.where(allow[:, ks - WINDOW:], s[:, ks - WINDOW:], MASK_VALUE)], axis=1)
                m = jnp.maximum(jnp.max(s, axis=-1, keepdims=True), sink)
                p = jnp.exp2(s - m)
                denom = jnp.sum(p, axis=-1, keepdims=True) + jnp.exp2(sink - m)
                ps.append(p.astype(BF16))
                inv.append(1.0 / denom)
            pv = _dot(jnp.concatenate(ps, axis=0), vv[sb * sub:sb * sub + ks])
            for pair in range(group // 2):
                even = pv[(2 * pair) * sub:(2 * pair + 1) * sub] * inv[2 * pair]
                odd = pv[(2 * pair + 1) * sub:(2 * pair + 2) * sub] * inv[2 * pair + 1]
                col = (group * h) // 2 + pair
                o_ref[0, rows, col * LANES:(col + 1) * LANES] = jnp.where(low_head, even, odd).astype(BF16)


def _attention(q, kv, sink):
    b, seq, _ = q.shape
    bq = ATTN_BQ
    assert seq % bq == 0 and bq % WINDOW == 0
    r = bq // WINDOW
    nwb = seq // WINDOW
    side = lambda lane_blk, f: pl.BlockSpec((1, WINDOW, 2 * LANES), lambda bi, i: (bi, f(i), lane_blk))
    mid = lambda lane_blk: pl.BlockSpec((1, bq, 2 * LANES), lambda bi, i: (bi, i, lane_blk))
    left = lambda i: jnp.maximum(i * r - 1, 0)
    right = lambda i: jnp.minimum((i + 1) * r, nwb - 1)
    return pl.pallas_call(
        functools.partial(_attn_kernel, seq),
        grid=(b, seq // bq),
        in_specs=[pl.BlockSpec(memory_space=pltpu.SMEM),
                  pl.BlockSpec((1, bq, ATTN_WIDTH), lambda bi, i: (bi, i, 0)),
                  side(0, left), mid(0), side(0, right), side(1, left), mid(1), side(1, right)],
        out_specs=pl.BlockSpec((1, bq, ATTN_WIDTH), lambda bi, i: (bi, i, 0)),
        out_shape=jax.ShapeDtypeStruct((b, seq, ATTN_WIDTH), BF16),
        compiler_params=pltpu.CompilerParams(dimension_semantics=("arbitrary", "arbitrary"),
                                             vmem_limit_bytes=VMEM_LIMIT),
        name="window_attn",
    )(sink, q, kv, kv, kv, kv, kv, kv)


def _ssd_kernel(x_ref, bm_ref, cm_ref, z_ref, dt_ref, cwx_ref, cwb_ref, cwc_ref, cbx_ref, cbb_ref, cbc_ref,
                gp_ref, dexp_ref, ng_ref, tril_ref, triu_ref, ef_ref, eb_ref, sh_ref, y_ref, ss_ref,
                pad_sc, xs_sc, bs_sc, cs_sc, bt_sc, yacc_sc, locb_sc, ef_sc, er_sc, keepf_sc, keepb_sc, fr_sc, rows_sc,
                sf_sc, sb_sc):
    seq = x_ref.shape[1]
    nchunks = seq // CHUNK
    gw = GROUP_WIDTH
    hp = HEADS_PER_GROUP
    pad = CONV_PAD
    taps = [k for k in range(SSD_CONV) if k != SSD_CONV // 2]

    zeros = jnp.zeros((pad, gw + 2 * LANES), BF16)
    pad_sc[0:pad, :] = zeros
    pad_sc[pad + seq:pad + seq + pad, :] = zeros

    def fill(c, carry):
        r0 = pl.multiple_of(c * CHUNK, CHUNK)
        pad_sc[pl.ds(pad + r0, CHUNK), 0:gw] = x_ref[0, pl.ds(r0, CHUNK), :]
        pad_sc[pl.ds(pad + r0, CHUNK), gw:gw + LANES] = bm_ref[0, pl.ds(r0, CHUNK), :]
        pad_sc[pl.ds(pad + r0, CHUNK), gw + LANES:gw + 2 * LANES] = cm_ref[0, pl.ds(r0, CHUNK), :]
        return carry

    lax.fori_loop(0, nchunks, fill, 0)

    lane_row = lax.broadcasted_iota(jnp.int32, (1, LANES), 1)
    dt_bias = gp_ref[0, 0:1, :]
    a_row = jnp.where(lane_row < 2 * hp, -jnp.exp(gp_ref[0, 1:2, :]), 0.0)
    dexp = dexp_ref[...]
    tril = tril_ref[...]
    triu = triu_ref[...]
    qi = lax.broadcasted_iota(jnp.int32, (CHUNK, CHUNK), 0)
    si = lax.broadcasted_iota(jnp.int32, (CHUNK, CHUNK), 1)
    lower = si <= qi
    before = si < qi
    after = si > qi
    low_head = si < SSD_HEAD_DIM
    fwd_lane = si < hp

    def blockdiag(t):
        return jnp.concatenate([jnp.where(low_head, t, 0.0), jnp.where(low_head, 0.0, t)], axis=0).astype(BF16)

    def head_rows(v, e_ref):
        return _dot(_split3(jnp.broadcast_to(v, (SUBLANES, LANES))), e_ref[...])

    def chunk_scalars(r0):
        x = dt_ref[0, pl.ds(r0, CHUNK), :] + dt_bias
        dt = jnp.maximum(x, 0.0) + jnp.log1p(jnp.exp(-jnp.abs(x)))
        pieces = _split3(dt * a_row)
        fp = _dot(tril, pieces)
        rp = _dot(triu, pieces)
        f = fp[:, 0:LANES] + fp[:, LANES:2 * LANES] + fp[:, 2 * LANES:3 * LANES]
        r = rp[:, 0:LANES] + rp[:, LANES:2 * LANES] + rp[:, 2 * LANES:3 * LANES]
        return dt, f, r

    def expand(v, first):
        tiles = [jnp.where(low_head, v[:, first + 2 * p:first + 2 * p + 1], v[:, first + 2 * p + 1:first + 2 * p + 2])
                 for p in range(hp // 2)]
        return jnp.concatenate(tiles, axis=1)

    def conv(c, carry):
        r0 = pl.multiple_of(c * CHUNK, CHUNK)
        shifted = _dot(sh_ref[...], pad_sc[pl.ds(r0, CHUNK + 2 * pad), :])
        centre = pad_sc[pl.ds(r0 + pad, CHUNK), :].astype(F32)

        def one(lo, hi, w_ref, b_ref):
            acc = b_ref[...] + w_ref[SSD_CONV // 2:SSD_CONV // 2 + 1, :] * centre[:, lo:hi]
            for j, k in enumerate(taps):
                acc = acc + w_ref[k:k + 1, :] * shifted[j * CHUNK:(j + 1) * CHUNK, lo:hi]
            return _silu(acc)

        xs_sc[pl.ds(r0, CHUNK), :] = one(0, gw, cwx_ref, cbx_ref)
        bconv = one(gw, gw + LANES, cwb_ref, cbb_ref)
        bs_sc[pl.ds(r0, CHUNK), :] = bconv.astype(BF16)
        bt_sc[c] = bconv.T
        cs_sc[pl.ds(r0, CHUNK), :] = one(gw + LANES, gw + 2 * LANES, cwc_ref, cbc_ref).astype(BF16)

        dt, f, r = chunk_scalars(r0)
        fr = jnp.where(fwd_lane, f, r)
        fr_sc[c] = fr * LOG2E
        pt = jnp.where(si < 2 * hp, fr, dt).T
        cums_t = pt[0:2 * hp, :]
        dt_t = pt[2 * hp:4 * hp, :]
        dte_t = jnp.exp(jnp.where(qi[0:2 * hp] < hp, cums_t[:, CHUNK - 1:CHUNK] - cums_t, cums_t[:, 0:1] - cums_t)) * dt_t
        rows_sc[c, 0:2 * hp, :] = cums_t * LOG2E
        rows_sc[c, 2 * hp:4 * hp, :] = dt_t
        rows_sc[c, 4 * hp:6 * hp, :] = dte_t
        keepf_sc[c] = head_rows(jnp.exp(f[CHUNK - 1:CHUNK, :]), ef_ref)
        keepb_sc[c] = head_rows(jnp.exp(r[0:1, :]), eb_ref)
        ef_sc[c] = _dot(_split3(jnp.exp(f)), ef_ref[...])
        er_sc[c] = expand(jnp.exp(r), hp)
        return carry

    lax.fori_loop(0, nchunks, conv, 0, unroll=8)

    sf_sc[...] = jnp.zeros_like(sf_sc)
    sb_sc[...] = jnp.zeros_like(sb_sc)

    def ascend(c, carry):
        r0 = pl.multiple_of(c * CHUNK, CHUNK)
        fr = fr_sc[c]
        cums_t = rows_sc[c, 0:2 * hp, :]
        dt_t = rows_sc[c, 2 * hp:4 * hp, :]
        dte_t = rows_sc[c, 4 * hp:6 * hp, :]
        keep_f = keepf_sc[c]
        cc = cs_sc[pl.ds(r0, CHUNK), :]
        cb = _dot_nt(cc, bs_sc[pl.ds(r0, CHUNK), :])
        yacc_sc[pl.ds(r0, CHUNK), :] = _dot(cc, sf_sc[...].astype(BF16)) * ef_sc[c]
        bt = bt_sc[c]
        for pair in range(hp // 2):
            ms, bfs, bbs = [], [], []
            for i in (2 * pair, 2 * pair + 1):
                seg = jnp.where(lower, fr[:, i:i + 1] - cums_t[i:i + 1, :], fr[:, hp + i:hp + i + 1] - cums_t[hp + i:hp + i + 1, :])
                dtf_row = dt_t[i:i + 1, :]
                dtb_row = dt_t[hp + i:hp + i + 1, :]
                w = jnp.where(before, dtf_row, jnp.where(after, dtb_row, dtf_row + dtb_row))
                ms.append((cb * jnp.exp2(seg) * w).astype(BF16))
                bfs.append((bt * dte_t[i:i + 1, :]).astype(BF16))
                bbs.append((bt * dte_t[hp + i:hp + i + 1, :]).astype(BF16))
            sl = slice(pair * LANES, (pair + 1) * LANES)
            lhs = jnp.concatenate([jnp.concatenate(ms, axis=1), jnp.concatenate(bfs, axis=1),
                                   jnp.concatenate(bbs, axis=1)], axis=0)
            xs_pair = xs_sc[pl.ds(r0, CHUNK), sl]
            res = _dot(lhs, blockdiag(xs_pair))
            yacc_sc[pl.ds(r0, CHUNK), sl] = yacc_sc[pl.ds(r0, CHUNK), sl] + res[0:CHUNK] + xs_pair * dexp[:, sl]
            sf_sc[:, sl] = sf_sc[:, sl] * keep_f[0:1, sl] + res[CHUNK:2 * CHUNK]
            locb_sc[c, :, sl] = res[2 * CHUNK:3 * CHUNK]
        return carry

    lax.fori_loop(0, nchunks, ascend, 0, unroll=8)

    def descend(j, carry):
        c = nchunks - 1 - j
        r0 = pl.multiple_of(c * CHUNK, CHUNK)
        sb = sb_sc[...]
        y = yacc_sc[pl.ds(r0, CHUNK), :] + _dot(cs_sc[pl.ds(r0, CHUNK), :], sb.astype(BF16)) * er_sc[c]
        sb_sc[...] = sb * keepb_sc[c][0:1, :] + locb_sc[c]
        gated = y * _silu(z_ref[0, pl.ds(r0, CHUNK), :].astype(F32))
        ss = jnp.sum(gated * gated, axis=1, keepdims=True)
        ss_ref[0, pl.ds(r0, CHUNK), :] = jnp.broadcast_to(ss, (CHUNK, LANES))
        y_ref[0, pl.ds(r0, CHUNK), :] = (gated * ng_ref[...]).astype(BF16)
        return carry

    lax.fori_loop(0, nchunks, descend, 0, unroll=8)


def _ssd_constants():
    idx = np.arange(CHUNK)
    tril = (idx[None, :] <= idx[:, None]).astype(np.float32)
    triu = (idx[None, :] >= idx[:, None]).astype(np.float32)
    rows = np.arange(3 * LANES)[:, None] % LANES
    cols = np.arange(GROUP_WIDTH)[None, :] // SSD_HEAD_DIM
    ef = (rows == cols).astype(np.float32)
    eb = (rows == cols + HEADS_PER_GROUP).astype(np.float32)
    taps = [k for k in range(SSD_CONV) if k != SSD_CONV // 2]
    src = np.arange(CHUNK + 2 * CONV_PAD)[None, :]
    shift = np.concatenate([(src == idx[:, None] + CONV_PAD + k - SSD_CONV // 2) for k in taps], axis=0).astype(np.float32)
    return tuple(jnp.asarray(a, dtype=BF16) for a in (tril, triu, ef, eb, shift))


def _ssd(xbc, z, dt, conv_w, conv_b, gp, d_exp, norm_g):
    b, seq, _ = xbc.shape
    assert seq % CHUNK == 0
    gw = GROUP_WIDTH
    nx = SSD_WIDTH // LANES
    nchunks = seq // CHUNK
    consts = _ssd_constants()
    seq_blk = lambda w, f: pl.BlockSpec((1, seq, w), lambda bi, g: (bi, 0, f(g)))
    par = lambda rows, w, f: pl.BlockSpec((rows, w), lambda bi, g: (0, f(g)))
    full = lambda a: pl.BlockSpec(a.shape, lambda bi, g: (0,) * a.ndim)
    return pl.pallas_call(
        _ssd_kernel,
        grid=(b, SSD_GROUPS),
        in_specs=[seq_blk(gw, lambda g: g), seq_blk(LANES, lambda g: nx + g), seq_blk(LANES, lambda g: nx + SSD_GROUPS + g),
                  seq_blk(gw, lambda g: g), seq_blk(LANES, lambda g: g),
                  par(SSD_CONV, gw, lambda g: g), par(SSD_CONV, LANES, lambda g: nx + g),
                  par(SSD_CONV, LANES, lambda g: nx + SSD_GROUPS + g),
                  par(1, gw, lambda g: g), par(1, LANES, lambda g: nx + g), par(1, LANES, lambda g: nx + SSD_GROUPS + g),
                  pl.BlockSpec((1, SUBLANES, LANES), lambda bi, g: (g, 0, 0)), par(1, gw, lambda g: g), par(1, gw, lambda g: g)]
                 + [full(a) for a in consts],
        out_specs=[seq_blk(gw, lambda g: g), seq_blk(LANES, lambda g: g)],
        out_shape=[jax.ShapeDtypeStruct((b, seq, SSD_WIDTH), BF16), jax.ShapeDtypeStruct((b, seq, SSD_GROUPS * LANES), F32)],
        scratch_shapes=[pltpu.VMEM((seq + 2 * CONV_PAD, gw + 2 * LANES), BF16), pltpu.VMEM((seq, gw), F32),
                        pltpu.VMEM((seq, LANES), BF16), pltpu.VMEM((seq, LANES), BF16),
                        pltpu.VMEM((nchunks, SSD_STATE, CHUNK), F32), pltpu.VMEM((seq, gw), F32),
                        pltpu.VMEM((nchunks, SSD_STATE, gw), F32), pltpu.VMEM((nchunks, CHUNK, gw), F32),
                        pltpu.VMEM((nchunks, CHUNK, gw), F32), pltpu.VMEM((nchunks, SUBLANES, gw), F32), pltpu.VMEM((nchunks, SUBLANES, gw), F32),
                        pltpu.VMEM((nchunks, CHUNK, LANES), F32), pltpu.VMEM((nchunks, 6 * HEADS_PER_GROUP, LANES), F32),
                        pltpu.VMEM((SSD_STATE, gw), F32), pltpu.VMEM((SSD_STATE, gw), F32)],
        compiler_params=pltpu.CompilerParams(dimension_semantics=("arbitrary", "arbitrary"),
                                             vmem_limit_bytes=VMEM_LIMIT),
        name="bidir_ssd",
    )(xbc, xbc, xbc, z, dt, conv_w, conv_w, conv_w, conv_b, conv_b, conv_b, gp, d_exp, norm_g, *consts)


def _outproj_kernel(attn_ref, y_ref, ss_ref, h_ref, wa_ref, ws_ref, g_ref, b_ref, o_ref):
    slab = h_ref.shape[0] // FFN_OUT_SLABS
    for s in range(FFN_OUT_SLABS):
        rows = slice(s * slab, (s + 1) * slab)
        total = ss_ref[rows, 0:1]
        for g in range(1, SSD_GROUPS):
            total = total + ss_ref[rows, g * LANES:g * LANES + 1]
        rs = lax.rsqrt(total * (1.0 / SSD_WIDTH) + EPS)
        mix = _dot(attn_ref[rows, :], wa_ref[...]) + rs * _dot(y_ref[rows, :], ws_ref[...])
        o_ref[rows, :] = _layer_norm(DEEPNORM_ALPHA * h_ref[rows, :] + mix, g_ref[...], b_ref[...])


def _outproj(attn, y, ss, h, wa, ws, g, b):
    t = h.shape[0]
    tm = TOKEN_TILE
    row = lambda w: pl.BlockSpec((tm, w), lambda i: (i, 0))
    full = lambda a: pl.BlockSpec(a.shape, lambda i: (0,) * a.ndim)
    return pl.pallas_call(
        _outproj_kernel,
        grid=(t // tm,),
        in_specs=[row(ATTN_WIDTH), row(SSD_WIDTH), row(SSD_GROUPS * LANES), row(D_MODEL), full(wa), full(ws), full(g), full(b)],
        out_specs=row(D_MODEL),
        out_shape=jax.ShapeDtypeStruct((t, D_MODEL), F32),
        compiler_params=pltpu.CompilerParams(dimension_semantics=("arbitrary",), vmem_limit_bytes=VMEM_LIMIT),
        name="out_proj_ln",
    )(attn, y, ss, h, wa, ws, g, b)


def _ffn_kernel(tiles_per_seq, hp_ref, h_ref, hn_ref, wup_ref, cw_ref, cb_ref, wdn_ref, g_ref, b_ref, o_ref,
                hb_sc, act_sc):
    tm = h_ref.shape[0]
    pos = pl.program_id(0) % tiles_per_seq
    h = h_ref[...]
    hb_sc[0:FF_HALO, :] = jnp.where(pos == 0, 0.0, hp_ref[...]).astype(BF16)
    hb_sc[FF_HALO:FF_HALO + tm, :] = h.astype(BF16)
    hb_sc[FF_HALO + tm:, :] = jnp.where(pos == tiles_per_seq - 1, 0.0, hn_ref[...]).astype(BF16)
    for j in range(D_FF // FF_TILE):
        cols = slice(j * FF_TILE, (j + 1) * FF_TILE)
        vcols = slice(D_FF + j * FF_TILE, D_FF + (j + 1) * FF_TILE)
        g_ext = _dot(hb_sc[...], wup_ref[:, cols])
        val = _dot(hb_sc[FF_HALO:FF_HALO + tm, :], wup_ref[:, vcols])
        gate = cb_ref[:, cols] + cw_ref[1:2, cols] * g_ext[FF_HALO:FF_HALO + tm]
        gate = gate + cw_ref[0:1, cols] * g_ext[FF_HALO - 1:FF_HALO - 1 + tm]
        gate = gate + cw_ref[2:3, cols] * g_ext[FF_HALO + 1:FF_HALO + 1 + tm]
        act_sc[:, cols] = (_silu(gate) * val).astype(BF16)
    slab = tm // FFN_OUT_SLABS
    for s in range(FFN_OUT_SLABS):
        rows = slice(s * slab, (s + 1) * slab)
        ffn = _dot(act_sc[rows, :], wdn_ref[...])
        o_ref[rows, :] = _layer_norm(DEEPNORM_ALPHA * h_ref[rows, :] + ffn, g_ref[...], b_ref[...])


def _ffn(h, seq, wup, cw, cb, wdn, g, b):
    t = h.shape[0]
    tm = FFN_TOKEN_TILE
    assert seq % tm == 0 and tm % FF_HALO == 0 and D_FF % FF_TILE == 0
    tiles_per_seq = seq // tm
    hb = tm // FF_HALO
    nhb = t // FF_HALO
    row = pl.BlockSpec((tm, D_MODEL), lambda i: (i, 0))
    prev = pl.BlockSpec((FF_HALO, D_MODEL), lambda i: (jnp.maximum(i * hb - 1, 0), 0))
    nxt = pl.BlockSpec((FF_HALO, D_MODEL), lambda i: (jnp.minimum((i + 1) * hb, nhb - 1), 0))
    full = lambda a: pl.BlockSpec(a.shape, lambda i: (0,) * a.ndim, pipeline_mode=pl.Buffered(1))
    return pl.pallas_call(
        functools.partial(_ffn_kernel, tiles_per_seq),
        grid=(t // tm,),
        in_specs=[prev, row, nxt, full(wup), full(cw), full(cb), full(wdn), full(g), full(b)],
        out_specs=row,
        out_shape=jax.ShapeDtypeStruct((t, D_MODEL), F32),
        scratch_shapes=[pltpu.VMEM((tm + 2 * FF_HALO, D_MODEL), BF16), pltpu.VMEM((tm, D_FF), BF16)],
        compiler_params=pltpu.CompilerParams(dimension_semantics=("arbitrary",), vmem_limit_bytes=VMEM_LIMIT),
        name="ffn_ln",
    )(h, h, h, wup, cw, cb, wdn, g, b)


def _rope_tables(seq):
    inv_freq = ROPE_THETA ** (-jnp.arange(0, HEAD_DIM, 2, dtype=F32) / HEAD_DIM)
    ang = jnp.arange(seq, dtype=F32)[:, None] * inv_freq[None, :]
    cos, sin = jnp.cos(ang), jnp.sin(ang)
    reps = LANES // HEAD_DIM
    return jnp.tile(jnp.concatenate([cos, cos], axis=1), (1, reps)), jnp.tile(jnp.concatenate([-sin, sin], axis=1), (1, reps))


def _per_group(fwd, bwd):
    hp = HEADS_PER_GROUP
    out = jnp.zeros((SSD_GROUPS, LANES), F32)
    for g in range(SSD_GROUPS):
        out = out.at[g, 0:hp].set(fwd[g * hp:(g + 1) * hp])
        out = out.at[g, hp:2 * hp].set(bwd[g * hp:(g + 1) * hp])
    return out


def _layer_params(w_in, ssd_a_log, ssd_dt_bias, ssd_d, w_out, w_up, w_down):
    o_k = ATTN_WIDTH + KV_WIDTH
    o_z = o_k + KV_WIDTH
    o_x = o_z + SSD_WIDTH
    o_dt = o_x + XBC_WIDTH
    hp = HEADS_PER_GROUP
    wdt_src = w_in[:, o_dt:]
    wdt = jnp.zeros((D_MODEL, SSD_GROUPS * LANES), F32)
    for g in range(SSD_GROUPS):
        wdt = wdt.at[:, g * LANES:g * LANES + hp].set(wdt_src[:, g * hp:(g + 1) * hp])
        wdt = wdt.at[:, g * LANES + hp:g * LANES + 2 * hp].set(wdt_src[:, SSD_HEADS + g * hp:SSD_HEADS + (g + 1) * hp])
        wdt = wdt.at[:, g * LANES + 2 * hp:g * LANES + 4 * hp].set(wdt[:, g * LANES:g * LANES + 2 * hp])
    gp = jnp.zeros((SSD_GROUPS, SUBLANES, LANES), F32)
    dt_bias = _per_group(ssd_dt_bias[0], ssd_dt_bias[1])
    gp = gp.at[:, 0, :].set(dt_bias.at[:, 2 * hp:4 * hp].set(dt_bias[:, 0:2 * hp]))
    gp = gp.at[:, 1, :].set(_per_group(ssd_a_log[0], ssd_a_log[1]))
    return dict(
        wqk=w_in[:, :o_k].astype(BF16), wv=w_in[:, o_k:o_z].astype(BF16), wz=w_in[:, o_z:o_x].astype(BF16),
        wx=w_in[:, o_x:o_dt].astype(BF16), wdt=wdt.astype(BF16), gp=gp,
        d_exp=jnp.repeat(ssd_d, SSD_HEAD_DIM)[None, :],
        wa=w_out[:ATTN_WIDTH].astype(BF16), ws=w_out[ATTN_WIDTH:].astype(BF16),
        wup=w_up.astype(BF16), wdn=w_down.astype(BF16))


def _trunk(x, ln_in_g, ln_in_b, layers):
    b, seq, _ = x.shape
    cos_t, sin_t = _rope_tables(seq)
    h = x.reshape(b * seq, D_MODEL)
    lng, lnb = ln_in_g[None, :], ln_in_b[None, :]
    for li, p in enumerate(layers):
        outs = _inproj(h, seq, li == 0, lng, lnb, p["wqk"], p["wv"], p["wz"], p["wx"], p["wdt"], cos_t, sin_t)
        if li == 0:
            h, outs = outs[0], outs[1:]
        q, kv, z, xbc, dt = outs
        three = lambda a: a.reshape(b, seq, a.shape[-1])
        attn = _attention(three(q), three(kv), p["sink"])
        y, ss = _ssd(three(xbc), three(z), three(dt), p["conv_w"], p["conv_b"], p["gp"], p["d_exp"], p["norm_g"])
        h = _outproj(attn.reshape(b * seq, -1), y.reshape(b * seq, -1), ss.reshape(b * seq, -1), h,
                     p["wa"], p["ws"], p["ln1_g"], p["ln1_b"])
        h = _ffn(h, seq, p["wup"], p["ffn_cw"], p["ffn_cb"], p["wdn"], p["ln2_g"], p["ln2_b"])
    return h.reshape(b, seq, D_MODEL)


def kernel(x_prompt, x_sample, ln_in_g, ln_in_b, w_in, attn_sink, ssd_conv_w, ssd_conv_b, ssd_a_log, ssd_dt_bias, ssd_d, ssd_norm_g, w_out, ln1_g, ln1_b, w_up, ffn_conv_w, ffn_conv_b, w_down, ln2_g, ln2_b):
    layers = []
    for i in range(w_in.shape[0]):
        p = _layer_params(w_in[i], ssd_a_log[i], ssd_dt_bias[i], ssd_d[i], w_out[i], w_up[i], w_down[i])
        p.update(sink=attn_sink[i], conv_w=ssd_conv_w[i], conv_b=ssd_conv_b[i][None, :], norm_g=ssd_norm_g[i][None, :],
                 ln1_g=ln1_g[i][None, :], ln1_b=ln1_b[i][None, :], ffn_cw=ffn_conv_w[i], ffn_cb=ffn_conv_b[i][None, :],
                 ln2_g=ln2_g[i][None, :], ln2_b=ln2_b[i][None, :])
        layers.append(p)
    return (_trunk(x_prompt, ln_in_g, ln_in_b, layers), _trunk(x_sample, ln_in_g, ln_in_b, layers))
```
